```python
import math
import jax, jax.numpy as jnp
from jax import lax
import numpy as np


D_MODEL = 1024
BATCH = 16
SEQ = 4096
DEPTH = 2
DEC_BATCH = 2
DEC_SEQ = 16384
PAST_LEN = 128

GRID_W = 64
NA_HEADS = 16
NA_HEAD_DIM = D_MODEL // NA_HEADS
NA_KH_MAX = 8
NA_KW = 16
NA_QCB = NA_KW
NA_KCB = 2 * NA_KW
DIFF_HEAD_DIM = 64
DIFF_HEADS = D_MODEL // (2 * DIFF_HEAD_DIM)
Q_BLOCK = 128
T5_BUCKETS = 32
T5_MAX_DIST = 128
D_FF = -(-(8 * D_MODEL) // (3 * 256)) * 256
N_NA_LAYERS = (DEPTH + 1) // 2
N_DIFF_LAYERS = DEPTH // 2
RMS_EPS = 1e-6
NEG_INF = -1e30

kernel_name = "hybrid_natten_diffattn_encoder"


def rms_norm(x, g):
    xf = x.astype(jnp.float32)
    y = xf * lax.rsqrt(jnp.mean(xf * xf, axis=-1, keepdims=True) + RMS_EPS)
    return (y * g.astype(jnp.float32)).astype(x.dtype)


def swiglu(x, w_gate, w_up, w_down):
    return (jax.nn.silu(x @ w_gate) * (x @ w_up)) @ w_down


def neighborhood_attention(x, w_qkv, w_o, rpb):
    b, s, _ = x.shape
    rows = s // GRID_W
    kh = min(NA_KH_MAX, rows)
    ncb = GRID_W // NA_QCB
    qkv = (x @ w_qkv).reshape(b, rows, GRID_W, 3, NA_HEADS, NA_HEAD_DIM)
    q = qkv[:, :, :, 0] * (NA_HEAD_DIM ** -0.5)
    k = qkv[:, :, :, 1]
    v = qkv[:, :, :, 2]
    r = jnp.arange(rows)
    row0 = jnp.clip(r - kh // 2, 0, rows - kh)
    key_rows = row0[:, None] + jnp.arange(kh)[None, :]
    dr_idx = key_rows - r[:, None] + (NA_KH_MAX - 1)
    j = jnp.arange(ncb)
    q_cols = j[:, None] * NA_QCB + jnp.arange(NA_QCB)[None, :]
    col0 = jnp.clip(j * NA_QCB - NA_KW // 2, 0, GRID_W - NA_KCB)
    k_cols = col0[:, None] + jnp.arange(NA_KCB)[None, :]
    q_start = jnp.clip(q_cols - NA_KW // 2, 0, GRID_W - NA_KW)
    kc = k_cols[:, None, :]
    in_win = (kc >= q_start[:, :, None]) & (kc < q_start[:, :, None] + NA_KW)
    dc_idx = jnp.clip(kc - q_cols[:, :, None] + (NA_KW - 1), 0, 2 * NA_KW - 2)
    rpb_f = rpb.astype(jnp.float32)

    def row_step(args):
        q_r, rows_r, dr_r = args
        k_blk = k[:, rows_r[:, None, None], k_cols[None, :, :]]
        v_blk = v[:, rows_r[:, None, None], k_cols[None, :, :]]
        bias = rpb_f[:, dr_r][:, :, dc_idx]
        bias = jnp.transpose(bias, (0, 2, 3, 1, 4))
        bias = jnp.where(in_win[:, :, None, :], bias, NEG_INF)
        sc = jnp.einsum('bjqhd,bkjchd->bhjqkc', q_r, k_blk).astype(jnp.float32) + bias
        p = jax.nn.softmax(sc.reshape(b, NA_HEADS, ncb, NA_QCB, kh * NA_KCB), axis=-1)
        p = p.reshape(sc.shape).astype(v.dtype)
        return jnp.einsum('bhjqkc,bkjchd->bjqhd', p, v_blk)

    q_rows = jnp.moveaxis(q.reshape(b, rows, ncb, NA_QCB, NA_HEADS, NA_HEAD_DIM), 1, 0)
    out = lax.map(row_step, (q_rows, key_rows, dr_idx))
    out = jnp.moveaxis(out, 0, 1).reshape(b, s, D_MODEL)
    return out @ w_o


def t5_bucket(rel):
    nb = T5_BUCKETS // 2
    max_exact = nb // 2
    ret = jnp.where(rel > 0, nb, 0)
    n = jnp.abs(rel)
    nf = jnp.maximum(n, max_exact).astype(jnp.float32)
    large = max_exact + (jnp.log(nf / max_exact) / math.log(T5_MAX_DIST / max_exact)
                         * (nb - max_exact)).astype(jnp.int32)
    large = jnp.minimum(large, nb - 1)
    return ret + jnp.where(n < max_exact, n, large)


def diff_attention(x, w_qkv, w_o, lq1, lk1, lq2, lk2, subln_g, t5_bias, lambda_init):
    b, s, _ = x.shape
    nblk = s // Q_BLOCK
    q, k, v = jnp.split(x @ w_qkv, 3, axis=-1)
    q = (q * DIFF_HEAD_DIM ** -0.5).reshape(b, nblk, Q_BLOCK, DIFF_HEADS, 2, DIFF_HEAD_DIM)
    qb = jnp.transpose(q, (1, 4, 0, 3, 2, 5))
    k = k.reshape(b, s, DIFF_HEADS, 2, DIFF_HEAD_DIM)
    k1 = jnp.transpose(k[:, :, :, 0], (0, 2, 1, 3))
    k2 = jnp.transpose(k[:, :, :, 1], (0, 2, 1, 3))
    v = jnp.transpose(v.reshape(b, s, DIFF_HEADS, 2 * DIFF_HEAD_DIM), (0, 2, 1, 3))
    f32 = jnp.float32
    lam = (jnp.exp(jnp.sum(lq1.astype(f32) * lk1.astype(f32)))
           - jnp.exp(jnp.sum(lq2.astype(f32) * lk2.astype(f32))) + lambda_init)
    k_pos = jnp.arange(s)
    table = t5_bias.astype(f32)

    def blk_step(args):
        q_blk, i = args
        q_pos = i * Q_BLOCK + jnp.arange(Q_BLOCK)
        bias = jnp.transpose(table[t5_bucket(k_pos[None, :] - q_pos[:, None])], (2, 0, 1))
        a1 = jax.nn.softmax(jnp.einsum('bhqd,bhkd->bhqk', q_blk[0], k1).astype(f32) + bias, axis=-1)
        a2 = jax.nn.softmax(jnp.einsum('bhqd,bhkd->bhqk', q_blk[1], k2).astype(f32) + bias, axis=-1)
        w = (a1 - lam * a2).astype(v.dtype)
        return jnp.einsum('bhqk,bhke->bhqe', w, v)

    out = lax.map(blk_step, (qb, jnp.arange(nblk)))
    out = rms_norm(out, subln_g) * (1.0 - lambda_init)
    out = jnp.transpose(out, (1, 0, 3, 2, 4)).reshape(b, s, D_MODEL)
    return out @ w_o


def encoder_trunk(x, mix_pre_g, mix_post_g, na_w_qkv, na_w_o, na_rpb, diff_w_qkv, diff_w_o,
                  diff_lambda_q1, diff_lambda_k1, diff_lambda_q2, diff_lambda_k2, diff_subln_g,
                  t5_bias, ffn_pre_g, ffn_post_g, ffn_w_gate, ffn_w_up, ffn_w_down):
    for i in range(DEPTH):
        h = rms_norm(x, mix_pre_g[i])
        li = i // 2
        if i % 2 == 0:
            m = neighborhood_attention(h, na_w_qkv[li], na_w_o[li], na_rpb[li])
        else:
            lambda_init = 0.8 - 0.6 * math.exp(-0.3 * i)
            m = diff_attention(h, diff_w_qkv[li], diff_w_o[li], diff_lambda_q1[li], diff_lambda_k1[li],
                               diff_lambda_q2[li], diff_lambda_k2[li], diff_subln_g[li], t5_bias, lambda_init)
        x = x + rms_norm(m, mix_post_g[i])
        h = rms_norm(x, ffn_pre_g[i])
        x = x + rms_norm(swiglu(h, ffn_w_gate[i], ffn_w_up[i], ffn_w_down[i]), ffn_post_g[i])
    return x


def setup_inputs(seed: int = 0) -> dict:
    key = jax.random.key(seed)
    ks = jax.random.split(key, 20)
    n = jax.random.normal
    f32 = jnp.float32
    D = D_MODEL
    return {
        "x_prompt": n(ks[0], (BATCH, SEQ, D), f32),
        "x_sample": n(ks[1], (DEC_BATCH, DEC_SEQ, D), f32),
        "mix_pre_g": 1.0 + 0.05 * n(ks[2], (DEPTH, D), f32),
        "mix_post_g": 1.0 + 0.05 * n(ks[3], (DEPTH, D), f32),
        "na_w_qkv": n(ks[4], (N_NA_LAYERS, D, 3 * D), f32) * D ** -0.5,
        "na_w_o": n(ks[5], (N_NA_LAYERS, D, D), f32) * D ** -0.5,
        "na_rpb": 0.2 * n(ks[6], (N_NA_LAYERS, NA_HEADS, 2 * NA_KH_MAX - 1, 2 * NA_KW - 1), f32),
        "diff_w_qkv": n(ks[7], (N_DIFF_LAYERS, D, 3 * D), f32) * D ** -0.5,
        "diff_w_o": n(ks[8], (N_DIFF_LAYERS, D, D), f32) * D ** -0.5,
        "diff_lambda_q1": 0.1 * n(ks[9], (N_DIFF_LAYERS, DIFF_HEAD_DIM), f32),
        "diff_lambda_k1": 0.1 * n(ks[10], (N_DIFF_LAYERS, DIFF_HEAD_DIM), f32),
        "diff_lambda_q2": 0.1 * n(ks[11], (N_DIFF_LAYERS, DIFF_HEAD_DIM), f32),
        "diff_lambda_k2": 0.1 * n(ks[12], (N_DIFF_LAYERS, DIFF_HEAD_DIM), f32),
        "diff_subln_g": 1.0 + 0.05 * n(ks[13], (N_DIFF_LAYERS, 2 * DIFF_HEAD_DIM), f32),
        "t5_bias": 0.2 * n(ks[14], (T5_BUCKETS, DIFF_HEADS), f32),
        "ffn_pre_g": 1.0 + 0.05 * n(ks[15], (DEPTH, D), f32),
        "ffn_post_g": 1.0 + 0.05 * n(ks[16], (DEPTH, D), f32),
        "ffn_w_gate": n(ks[17], (DEPTH, D, D_FF), f32) * D ** -0.5,
        "ffn_w_up": n(ks[18], (DEPTH, D, D_FF), f32) * D ** -0.5,
        "ffn_w_down": n(ks[19], (DEPTH, D_FF, D), f32) * D_FF ** -0.5,
    }


def reference(x_prompt, x_sample, mix_pre_g, mix_post_g, na_w_qkv, na_w_o, na_rpb, diff_w_qkv, diff_w_o,
              diff_lambda_q1, diff_lambda_k1, diff_lambda_q2, diff_lambda_k2, diff_subln_g, t5_bias,
              ffn_pre_g, ffn_post_g, ffn_w_gate, ffn_w_up, ffn_w_down):
    y_prompt = encoder_trunk(x_prompt, mix_pre_g, mix_post_g, na_w_qkv, na_w_o, na_rpb, diff_w_qkv, diff_w_o,
                             diff_lambda_q1, diff_lambda_k1, diff_lambda_q2, diff_lambda_k2, diff_subln_g,
                             t5_bias, ffn_pre_g, ffn_post_g, ffn_w_gate, ffn_w_up, ffn_w_down)
    y_sample = encoder_trunk(x_sample, mix_pre_g, mix_post_g, na_w_qkv, na_w_o, na_rpb, diff_w_qkv, diff_w_o,
                             diff_lambda_q1, diff_lambda_k1, diff_lambda_q2, diff_lambda_k2, diff_subln_g,
                             t5_bias, ffn_pre_g, ffn_post_g, ffn_w_gate, ffn_w_up, ffn_w_down)
    return (y_prompt, y_sample)
```

```python
import functools
import math

import jax
import jax.numpy as jnp
from jax import lax
from jax.experimental import pallas as pl
from jax.experimental.pallas import tpu as pltpu

F32 = jnp.float32
BF16 = jnp.bfloat16

D_MODEL = 1024
DEPTH = 2
RMS_EPS = 1e-6
NEG_INF = -1e30

GRID_W = 64
NA_HEADS = 16
NA_HEAD_DIM = 64
NA_KH = 8
NA_KW = 16
NA_RQ = 4
NA_WK = NA_RQ + NA_KH - 1
NA_TQ = NA_RQ * GRID_W
NA_TK = NA_WK * GRID_W

DIFF_HEADS = 8
DIFF_HEAD_DIM = 64
DIFF_T = 512
T5_BUCKETS = 32
T5_MAX_DIST = 128

HEAD_LANES = 128
D_FF = 2816

TM_QKV = 512
TM_FFN = 256
VMEM_LIMIT = 56 * 1024 * 1024


def _rms(x, g):
    return x * lax.rsqrt(jnp.mean(x * x, axis=-1, keepdims=True) + RMS_EPS) * g


def _qkv_kernel(x_ref, g_ref, w_ref, o_ref):
    h = _rms(x_ref[...], g_ref[...]).astype(BF16)
    for c in range(3):
        cols = slice(c * D_MODEL, (c + 1) * D_MODEL)
        o_ref[:, cols] = jnp.dot(h, w_ref[:, cols], preferred_element_type=F32).astype(BF16)


def _qkv_proj(x2, g, w):
    n = x2.shape[0]
    return pl.pallas_call(
        _qkv_kernel,
        grid=(n // TM_QKV,),
        in_specs=[
            pl.BlockSpec((TM_QKV, D_MODEL), lambda i: (i, 0)),
            pl.BlockSpec((1, D_MODEL), lambda i: (0, 0)),
            pl.BlockSpec((D_MODEL, 3 * D_MODEL), lambda i: (0, 0)),
        ],
        out_specs=pl.BlockSpec((TM_QKV, 3 * D_MODEL), lambda i: (i, 0)),
        out_shape=jax.ShapeDtypeStruct((n, 3 * D_MODEL), BF16),
        compiler_params=pltpu.CompilerParams(
            dimension_semantics=("arbitrary",), vmem_limit_bytes=VMEM_LIMIT),
        name="qkv_proj",
    )(x2, g.reshape(1, D_MODEL), w)


def _post_ffn_kernel(a_ref, x_ref, wo_ref, gpost_ref, gfpre_ref, gfpost_ref,
                     wg_ref, wu_ref, wd_ref, o_ref):
    m = jnp.dot(a_ref[...], wo_ref[...], preferred_element_type=F32)
    x1 = x_ref[...] + _rms(m, gpost_ref[...])
    h = _rms(x1, gfpre_ref[...]).astype(BF16)
    gate = jnp.dot(h, wg_ref[...], preferred_element_type=F32)
    up = jnp.dot(h, wu_ref[...], preferred_element_type=F32)
    act = (gate * (1.0 / (1.0 + jnp.exp(-gate))) * up).astype(BF16)
    f = jnp.dot(act, wd_ref[...], preferred_element_type=F32)
    o_ref[...] = x1 + _rms(f, gfpost_ref[...])


def _post_ffn(att2, x2, wo, gpost, gfpre, gfpost, wg, wu, wd):
    n = x2.shape[0]
    row = lambda i: (i, 0)
    fixed = lambda i: (0, 0)
    once = pl.Buffered(1)
    vec = pl.BlockSpec((1, D_MODEL), fixed)
    return pl.pallas_call(
        _post_ffn_kernel,
        grid=(n // TM_FFN,),
        in_specs=[
            pl.BlockSpec((TM_FFN, D_MODEL), row),
            pl.BlockSpec((TM_FFN, D_MODEL), row),
            pl.BlockSpec((D_MODEL, D_MODEL), fixed, pipeline_mode=once),
            vec, vec, vec,
            pl.BlockSpec((D_MODEL, D_FF), fixed, pipeline_mode=once),
            pl.BlockSpec((D_MODEL, D_FF), fixed, pipeline_mode=once),
            pl.BlockSpec((D_FF, D_MODEL), fixed, pipeline_mode=once),
        ],
        out_specs=pl.BlockSpec((TM_FFN, D_MODEL), row),
        out_shape=jax.ShapeDtypeStruct((n, D_MODEL), F32),
        compiler_params=pltpu.CompilerParams(
            dimension_semantics=("arbitrary",), vmem_limit_bytes=VMEM_LIMIT),
        name="post_ffn",
    )(att2, x2, wo, gpost.reshape(1, D_MODEL), gfpre.reshape(1, D_MODEL),
      gfpost.reshape(1, D_MODEL), wg, wu, wd)


def _na_bias_tiles(rpb):
    rpb = rpb.astype(F32)
    a = jnp.arange(NA_RQ)
    w = jnp.arange(NA_WK)
    half = NA_KH // 2
    offs = (0, half, NA_WK - NA_RQ)
    los = (jnp.zeros_like(a), a, jnp.full_like(a, NA_WK - NA_KH))
    qc = jnp.arange(GRID_W)
    kc = jnp.arange(GRID_W)
    q_start = jnp.clip(qc - NA_KW // 2, 0, GRID_W - NA_KW)
    col_ok = (kc[None, :] >= q_start[:, None]) & (kc[None, :] < q_start[:, None] + NA_KW)
    dc = jnp.clip(kc[None, :] - qc[:, None] + (NA_KW - 1), 0, 2 * NA_KW - 2)
    by_col = rpb[:, :, dc]
    tiles = []
    for off, lo in zip(offs, los):
        row_ok = (w[None, :] >= lo[:, None]) & (w[None, :] < lo[:, None] + NA_KH)
        dr = jnp.clip(w[None, :] - (a[:, None] + off) + (NA_KH - 1), 0, 2 * NA_KH - 2)
        t = by_col[:, dr]
        ok = row_ok[:, :, None, None] & col_ok[None, None, :, :]
        t = jnp.where(ok[None], t, NEG_INF)
        t = jnp.transpose(t, (0, 1, 3, 2, 4)).reshape(NA_HEADS, NA_TQ, NA_TK)
        tiles.append(t)
    return jnp.stack(tiles)


def _na_kernel(q_ref, k_ref, v_ref, bias_ref, o_ref, *, rows):
    i = pl.program_id(2)
    nblk = pl.num_programs(2)
    key_row0 = jnp.clip(i * NA_RQ - NA_KH // 2, 0, rows - NA_WK)
    kstart = pl.multiple_of(key_row0 * GRID_W, GRID_W)
    cls = jnp.where(i == 0, 0, jnp.where(i == nblk - 1, 2, 1))
    q = q_ref[0]
    k = k_ref[0, pl.ds(kstart, NA_TK), :]
    v = v_ref[0, pl.ds(kstart, NA_TK), :]
    lane = lax.broadcasted_iota(jnp.int32, (NA_TQ, HEAD_LANES), 1)
    first = lane < NA_HEAD_DIM
    outs = []
    for hh in range(2):
        qh = jnp.where(first if hh == 0 else jnp.logical_not(first), q, jnp.zeros_like(q))
        s = lax.dot_general(qh, k, (((1,), (1,)), ((), ())), preferred_element_type=F32)
        s = s + bias_ref[cls, hh]
        m = jnp.max(s, axis=-1, keepdims=True)
        p = jnp.exp(s - m)
        l = jnp.sum(p, axis=-1, keepdims=True)
        o = jnp.dot(p.astype(BF16), v, preferred_element_type=F32)
        outs.append(o / l)
    o_ref[0] = jnp.where(first, outs[0], outs[1]).astype(BF16)


def _na_attention(qkv3, bias):
    b, s, _ = qkv3.shape
    rows = s // GRID_W
    nblk = rows // NA_RQ
    assert nblk >= 3 and rows >= NA_WK
    npair = NA_HEADS // 2
    return pl.pallas_call(
        functools.partial(_na_kernel, rows=rows),
        grid=(npair, b, nblk),
        in_specs=[
            pl.BlockSpec((1, NA_TQ, HEAD_LANES), lambda hp, bb, i: (bb, i, hp)),
            pl.BlockSpec((1, s, HEAD_LANES), lambda hp, bb, i: (bb, 0, npair + hp)),
            pl.BlockSpec((1, s, HEAD_LANES), lambda hp, bb, i: (bb, 0, 2 * npair + hp)),
            pl.BlockSpec((3, 2, NA_TQ, NA_TK), lambda hp, bb, i: (0, hp, 0, 0)),
        ],
        out_specs=pl.BlockSpec((1, NA_TQ, HEAD_LANES), lambda hp, bb, i: (bb, i, hp)),
        out_shape=jax.ShapeDtypeStruct((b, s, D_MODEL), BF16),
        compiler_params=pltpu.CompilerParams(
            dimension_semantics=("arbitrary", "arbitrary", "arbitrary"),
            vmem_limit_bytes=VMEM_LIMIT),
        name="na_attention",
    )(qkv3, qkv3, qkv3, bias)


def _t5_bucket(rel):
    nb = T5_BUCKETS // 2
    max_exact = nb // 2
    ret = jnp.where(rel > 0, nb, 0)
    n = jnp.abs(rel)
    nf = jnp.maximum(n, max_exact).astype(F32)
    large = max_exact + (jnp.log(nf / max_exact) / math.log(T5_MAX_DIST / max_exact)
                         * (nb - max_exact)).astype(jnp.int32)
    large = jnp.minimum(large, nb - 1)
    return ret + jnp.where(n < max_exact, n, large)


def _diff_bias_tiles(t5_bias):
    table = t5_bias.astype(F32)
    t = DIFF_T
    q = jnp.arange(t)
    tiles = []
    for d in (-1, 0, 1):
        rel = (d * t + jnp.arange(t))[None, :] - q[:, None]
        tiles.append(jnp.transpose(table[_t5_bucket(rel)], (2, 0, 1)))
    return jnp.stack(tiles, axis=1)


def _diff_kernel(far_ref, lam_ref, q_ref, k_ref, v_ref, nb_ref, g_ref, o_ref, *, nk, post_scale):
    t = DIFF_T
    h = pl.program_id(0)
    qi = pl.program_id(2)
    q = q_ref[0]
    lane = lax.broadcasted_iota(jnp.int32, (t, HEAD_LANES), 1)
    first = lane < DIFF_HEAD_DIM
    zero = jnp.zeros_like(q)
    qs = (jnp.where(first, q, zero), jnp.where(first, zero, q))

    def tile(j, carry, bias_tile, bias_const):
        k = k_ref[0, pl.ds(pl.multiple_of(j * t, t), t), :]
        v = v_ref[0, pl.ds(pl.multiple_of(j * t, t), t), :]
        new = []
        for c in range(2):
            m_old, l_old, acc_old = carry[3 * c:3 * c + 3]
            s = lax.dot_general(qs[c], k, (((1,), (1,)), ((), ())), preferred_element_type=F32)
            if bias_tile is not None:
                s = s + bias_tile
                m_new = jnp.maximum(m_old, jnp.max(s, axis=-1, keepdims=True))
                p = jnp.exp(s - m_new)
            else:
                m_new = jnp.maximum(m_old, jnp.max(s, axis=-1, keepdims=True) + bias_const)
                p = jnp.exp(s - (m_new - bias_const))
            alpha = jnp.exp(m_old - m_new)
            l_new = alpha * l_old + jnp.sum(p, axis=-1, keepdims=True)
            acc_new = alpha * acc_old + jnp.dot(p.astype(BF16), v, preferred_element_type=F32)
            new += [m_new, l_new, acc_new]
        return tuple(new)

    m0 = jnp.full((t, 1), NEG_INF, F32)
    l0 = jnp.zeros((t, 1), F32)
    a0 = jnp.zeros((t, HEAD_LANES), F32)
    carry = (m0, l0, a0, m0, l0, a0)
    near_lo = jnp.maximum(qi - 1, 0)
    near_hi = jnp.minimum(qi + 2, nk)
    c_left = far_ref[h, 0]
    c_right = far_ref[h, 1]
    carry = lax.fori_loop(0, near_lo, lambda j, c: tile(j, c, None, c_left), carry)
    carry = lax.fori_loop(near_lo, near_hi,
                          lambda j, c: tile(j, c, nb_ref[0, j - qi + 1], None), carry)
    carry = lax.fori_loop(near_hi, nk, lambda j, c: tile(j, c, None, c_right), carry)
    _, l1, acc1, _, l2, acc2 = carry
    out = acc1 / l1 - lam_ref[0] * (acc2 / l2)
    out = _rms(out, g_ref[...]) * post_scale
    o_ref[0] = out.astype(BF16)


def _diff_attention(qkv3, near_bias, far_bias, lam, subln_g, lambda_init):
    b, s, _ = qkv3.shape
    t = DIFF_T
    nk = s // t
    assert nk >= 1 and t > T5_MAX_DIST
    nh = DIFF_HEADS
    grid_spec = pltpu.PrefetchScalarGridSpec(
        num_scalar_prefetch=2,
        grid=(nh, b, nk),
        in_specs=[
            pl.BlockSpec((1, t, HEAD_LANES), lambda h, bb, i, *_: (bb, i, h)),
            pl.BlockSpec((1, s, HEAD_LANES), lambda h, bb, i, *_: (bb, 0, nh + h)),
            pl.BlockSpec((1, s, HEAD_LANES), lambda h, bb, i, *_: (bb, 0, 2 * nh + h)),
            pl.BlockSpec((1, 3, t, t), lambda h, bb, i, *_: (h, 0, 0, 0)),
            pl.BlockSpec((1, HEAD_LANES), lambda h, bb, i, *_: (0, 0)),
        ],
        out_specs=pl.BlockSpec((1, t, HEAD_LANES), lambda h, bb, i, *_: (bb, i, h)),
    )
    return pl.pallas_call(
        functools.partial(_diff_kernel, nk=nk, post_scale=1.0 - lambda_init),
        grid_spec=grid_spec,
        out_shape=jax.ShapeDtypeStruct((b, s, D_MODEL), BF16),
        compiler_params=pltpu.CompilerParams(
            dimension_semantics=("arbitrary", "arbitrary", "arbitrary"),
            vmem_limit_bytes=VMEM_LIMIT),
        name="diff_attention",
    )(far_bias, lam, qkv3, qkv3, qkv3, near_bias, subln_g.reshape(1, HEAD_LANES))


def _scaled_qkv_weight(w, head_dim):
    scale = head_dim ** -0.5
    assert math.frexp(scale)[0] == 0.5
    return jnp.concatenate([w[:, :D_MODEL] * scale, w[:, D_MODEL:]], axis=1).astype(BF16)


def kernel(x_prompt, x_sample, mix_pre_g, mix_post_g, na_w_qkv, na_w_o, na_rpb, diff_w_qkv, diff_w_o,
           diff_lambda_q1, diff_lambda_k1, diff_lambda_q2, diff_lambda_k2, diff_subln_g, t5_bias,
           ffn_pre_g, ffn_post_g, ffn_w_gate, ffn_w_up, ffn_w_down):
    na_bias = [_na_bias_tiles(na_rpb[li]) for li in range(na_rpb.shape[0])]
    diff_near = _diff_bias_tiles(t5_bias)
    table = t5_bias.astype(F32)
    far = jnp.stack([table[T5_BUCKETS // 2 - 1], table[T5_BUCKETS - 1]], axis=1)
    wg = ffn_w_gate.astype(BF16)
    wu = ffn_w_up.astype(BF16)
    wd = ffn_w_down.astype(BF16)

    def trunk(x):
        b, s, d = x.shape
        x2 = x.reshape(b * s, d)
        for i in range(DEPTH):
            li = i // 2
            if i % 2 == 0:
                w_qkv = _scaled_qkv_weight(na_w_qkv[li], NA_HEAD_DIM)
                qkv = _qkv_proj(x2, mix_pre_g[i], w_qkv).reshape(b, s, 3 * d)
                att = _na_attention(qkv, na_bias[li])
                w_o = na_w_o[li]
            else:
                lambda_init = 0.8 - 0.6 * math.exp(-0.3 * i)
                lam = (jnp.exp(jnp.sum(diff_lambda_q1[li].astype(F32) * diff_lambda_k1[li].astype(F32)))
                       - jnp.exp(jnp.sum(diff_lambda_q2[li].astype(F32) * diff_lambda_k2[li].astype(F32)))
                       + lambda_init).reshape(1)
                w_qkv = _scaled_qkv_weight(diff_w_qkv[li], DIFF_HEAD_DIM)
                qkv = _qkv_proj(x2, mix_pre_g[i], w_qkv).reshape(b, s, 3 * d)
                att = _diff_attention(qkv, diff_near, far, lam, diff_subln_g[li], lambda_init)
                w_o = diff_w_o[li]
            x2 = _post_ffn(att.reshape(b * s, d), x2, w_o.astype(BF16), mix_post_g[i],
                           ffn_pre_g[i], ffn_post_g[i], wg[i], wu[i], wd[i])
        return x2.reshape(b, s, d)

    return (trunk(x_prompt), trunk(x_sample))
```

```python
import functools
import math

import jax
import jax.numpy as jnp
from jax import lax
from jax.experimental import pallas as pl
from jax.experimental.pallas import tpu as pltpu

F32 = jnp.float32
BF16 = jnp.bfloat16

D_MODEL = 1024
DEPTH = 2
RMS_EPS = 1e-6
NEG_INF = -1e30

GRID_W = 64
NA_HEADS = 16
NA_HEAD_DIM = 64
NA_KH = 8
NA_KW = 16
NA_RQ = 4
NA_WK = NA_RQ + NA_KH - 1
NA_TQ = NA_RQ * GRID_W
NA_TK = NA_WK * GRID_W

DIFF_HEADS = 8
DIFF_HEAD_DIM = 64
DIFF_T = 512
T5_BUCKETS = 32
T5_MAX_DIST = 128
DIFF_NEAR = 2

HEAD_LANES = 128
D_FF = 2816

TM_QKV = 512
TM_FFN = 256
VMEM_LIMIT = 56 * 1024 * 1024

_NT = (((1,), (1,)), ((), ()))
LOG2E = math.log2(math.e)


def _rms(x, g):
    return x * lax.rsqrt(jnp.mean(x * x, axis=-1, keepdims=True) + RMS_EPS) * g


def _qkv_kernel(x_ref, g_ref, w_ref, o_ref):
    h = _rms(x_ref[...], g_ref[...]).astype(BF16)
    for c in range(3):
        cols = slice(c * D_MODEL, (c + 1) * D_MODEL)
        o_ref[:, cols] = jnp.dot(h, w_ref[:, cols], preferred_element_type=F32).astype(BF16)


def _qkv_proj(x2, g, w):
    n = x2.shape[0]
    return pl.pallas_call(
        _qkv_kernel,
        grid=(n // TM_QKV,),
        in_specs=[
            pl.BlockSpec((TM_QKV, D_MODEL), lambda i: (i, 0)),
            pl.BlockSpec((1, D_MODEL), lambda i: (0, 0)),
            pl.BlockSpec((D_MODEL, 3 * D_MODEL), lambda i: (0, 0)),
        ],
        out_specs=pl.BlockSpec((TM_QKV, 3 * D_MODEL), lambda i: (i, 0)),
        out_shape=jax.ShapeDtypeStruct((n, 3 * D_MODEL), BF16),
        compiler_params=pltpu.CompilerParams(
            dimension_semantics=("arbitrary",), vmem_limit_bytes=VMEM_LIMIT),
        name="qkv_proj",
    )(x2, g.reshape(1, D_MODEL), w)


def _qk_vt_kernel(x_ref, g_ref, wqk_ref, wvt_ref, qk_ref, vt_ref):
    h = _rms(x_ref[...], g_ref[...]).astype(BF16)
    for c, scale in enumerate((LOG2E, None)):
        cols = slice(c * D_MODEL, (c + 1) * D_MODEL)
        acc = jnp.dot(h, wqk_ref[:, cols], preferred_element_type=F32)
        qk_ref[:, cols] = (acc if scale is None else acc * scale).astype(BF16)
    vt_ref[0] = lax.dot_general(wvt_ref[...], h, _NT, preferred_element_type=F32).astype(BF16)


def _qk_vt_proj(x2, g, wqk, wvt):
    n = x2.shape[0]
    assert TM_QKV == DIFF_T
    return pl.pallas_call(
        _qk_vt_kernel,
        grid=(n // TM_QKV,),
        in_specs=[
            pl.BlockSpec((TM_QKV, D_MODEL), lambda i: (i, 0)),
            pl.BlockSpec((1, D_MODEL), lambda i: (0, 0)),
            pl.BlockSpec((D_MODEL, 2 * D_MODEL), lambda i: (0, 0)),
            pl.BlockSpec((D_MODEL, D_MODEL), lambda i: (0, 0)),
        ],
        out_specs=[
            pl.BlockSpec((TM_QKV, 2 * D_MODEL), lambda i: (i, 0)),
            pl.BlockSpec((1, D_MODEL, TM_QKV), lambda i: (i, 0, 0)),
        ],
        out_shape=[
            jax.ShapeDtypeStruct((n, 2 * D_MODEL), BF16),
            jax.ShapeDtypeStruct((n // TM_QKV, D_MODEL, TM_QKV), BF16),
        ],
        compiler_params=pltpu.CompilerParams(
            dimension_semantics=("arbitrary",), vmem_limit_bytes=VMEM_LIMIT),
        name="qk_vt_proj",
    )(x2, g.reshape(1, D_MODEL), wqk, wvt)


def _post_ffn_kernel(a_ref, x_ref, wo_ref, gpost_ref, gfpre_ref, gfpost_ref,
                     wg_ref, wu_ref, wd_ref, o_ref):
    m = jnp.dot(a_ref[...], wo_ref[...], preferred_element_type=F32)
    x1 = x_ref[...] + _rms(m, gpost_ref[...])
    h = _rms(x1, gfpre_ref[...]).astype(BF16)
    gate = jnp.dot(h, wg_ref[...], preferred_element_type=F32)
    up = jnp.dot(h, wu_ref[...], preferred_element_type=F32)
    act = (gate * (1.0 / (1.0 + jnp.exp(-gate))) * up).astype(BF16)
    f = jnp.dot(act, wd_ref[...], preferred_element_type=F32)
    o_ref[...] = x1 + _rms(f, gfpost_ref[...])


def _post_ffn(att2, x2, wo, gpost, gfpre, gfpost, wg, wu, wd):
    n = x2.shape[0]
    row = lambda i: (i, 0)
    fixed = lambda i: (0, 0)
    once = pl.Buffered(1)
    vec = pl.BlockSpec((1, D_MODEL), fixed)
    return pl.pallas_call(
        _post_ffn_kernel,
        grid=(n // TM_FFN,),
        in_specs=[
            pl.BlockSpec((TM_FFN, D_MODEL), row),
            pl.BlockSpec((TM_FFN, D_MODEL), row),
            pl.BlockSpec((D_MODEL, D_MODEL), fixed, pipeline_mode=once),
            vec, vec, vec,
            pl.BlockSpec((D_MODEL, D_FF), fixed, pipeline_mode=once),
            pl.BlockSpec((D_MODEL, D_FF), fixed, pipeline_mode=once),
            pl.BlockSpec((D_FF, D_MODEL), fixed, pipeline_mode=once),
        ],
        out_specs=pl.BlockSpec((TM_FFN, D_MODEL), row),
        out_shape=jax.ShapeDtypeStruct((n, D_MODEL), F32),
        compiler_params=pltpu.CompilerParams(
            dimension_semantics=("arbitrary",), vmem_limit_bytes=VMEM_LIMIT),
        name="post_ffn",
    )(att2, x2, wo, gpost.reshape(1, D_MODEL), gfpre.reshape(1, D_MODEL),
      gfpost.reshape(1, D_MODEL), wg, wu, wd)


def _na_bias_tiles(rpb):
    rpb = rpb.astype(F32)
    a = jnp.arange(NA_RQ)
    w = jnp.arange(NA_WK)
    half = NA_KH // 2
    offs = (0, half, NA_WK - NA_RQ)
    los = (jnp.zeros_like(a), a, jnp.full_like(a, NA_WK - NA_KH))
    qc = jnp.arange(GRID_W)
    kc = jnp.arange(GRID_W)
    q_start = jnp.clip(qc - NA_KW // 2, 0, GRID_W - NA_KW)
    col_ok = (kc[None, :] >= q_start[:, None]) & (kc[None, :] < q_start[:, None] + NA_KW)
    dc = jnp.clip(kc[None, :] - qc[:, None] + (NA_KW - 1), 0, 2 * NA_KW - 2)
    by_col = rpb[:, :, dc]
    tiles = []
    for off, lo in zip(offs, los):
        row_ok = (w[None, :] >= lo[:, None]) & (w[None, :] < lo[:, None] + NA_KH)
        dr = jnp.clip(w[None, :] - (a[:, None] + off) + (NA_KH - 1), 0, 2 * NA_KH - 2)
        t = by_col[:, dr]
        ok = row_ok[:, :, None, None] & col_ok[None, None, :, :]
        t = jnp.where(ok[None], t, NEG_INF)
        t = jnp.transpose(t, (0, 1, 3, 2, 4)).reshape(NA_HEADS, NA_TQ, NA_TK)
        tiles.append(t)
    return jnp.stack(tiles)


def _na_kernel(q_ref, k_ref, v_ref, bias_ref, o_ref, *, rows):
    i = pl.program_id(2)
    nblk = pl.num_programs(2)
    key_row0 = jnp.clip(i * NA_RQ - NA_KH // 2, 0, rows - NA_WK)
    kstart = pl.multiple_of(key_row0 * GRID_W, GRID_W)
    cls = jnp.where(i == 0, 0, jnp.where(i == nblk - 1, 2, 1))
    q = q_ref[0]
    k = k_ref[0, pl.ds(kstart, NA_TK), :]
    v = v_ref[0, pl.ds(kstart, NA_TK), :]
    lane = lax.broadcasted_iota(jnp.int32, (NA_TQ, HEAD_LANES), 1)
    first = lane < NA_HEAD_DIM
    outs = []
    for hh in range(2):
        qh = jnp.where(first if hh == 0 else jnp.logical_not(first), q, jnp.zeros_like(q))
        s = lax.dot_general(qh, k, _NT, preferred_element_type=F32)
        s = s + bias_ref[cls, hh]
        m = jnp.max(s, axis=-1, keepdims=True)
        p = jnp.exp(s - m)
        l = jnp.sum(p, axis=-1, keepdims=True)
        o = jnp.dot(p.astype(BF16), v, preferred_element_type=F32)
        outs.append(o / l)
    o_ref[0] = jnp.where(first, outs[0], outs[1]).astype(BF16)


def _na_attention(qkv3, bias):
    b, s, _ = qkv3.shape
    rows = s // GRID_W
    nblk = rows // NA_RQ
    assert nblk >= 3 and rows >= NA_WK
    npair = NA_HEADS // 2
    return pl.pallas_call(
        functools.partial(_na_kernel, rows=rows),
        grid=(npair, b, nblk),
        in_specs=[
            pl.BlockSpec((1, NA_TQ, HEAD_LANES), lambda hp, bb, i: (bb, i, hp)),
            pl.BlockSpec((1, s, HEAD_LANES), lambda hp, bb, i: (bb, 0, npair + hp)),
            pl.BlockSpec((1, s, HEAD_LANES), lambda hp, bb, i: (bb, 0, 2 * npair + hp)),
            pl.BlockSpec((3, 2, NA_TQ, NA_TK), lambda hp, bb, i: (0, hp, 0, 0)),
        ],
        out_specs=pl.BlockSpec((1, NA_TQ, HEAD_LANES), lambda hp, bb, i: (bb, i, hp)),
        out_shape=jax.ShapeDtypeStruct((b, s, D_MODEL), BF16),
        compiler_params=pltpu.CompilerParams(
            dimension_semantics=("arbitrary", "arbitrary", "arbitrary"),
            vmem_limit_bytes=VMEM_LIMIT),
        name="na_attention",
    )(qkv3, qkv3, qkv3, bias)


def _t5_bucket(rel):
    nb = T5_BUCKETS // 2
    max_exact = nb // 2
    ret = jnp.where(rel > 0, nb, 0)
    n = jnp.abs(rel)
    nf = jnp.maximum(n, max_exact).astype(F32)
    large = max_exact + (jnp.log(nf / max_exact) / math.log(T5_MAX_DIST / max_exact)
                         * (nb - max_exact)).astype(jnp.int32)
    large = jnp.minimum(large, nb - 1)
    return ret + jnp.where(n < max_exact, n, large)


def _diff_bias_tiles(t5_bias):
    table = t5_bias.astype(F32)
    t = DIFF_T
    i = jnp.arange(2 * t)
    tiles = []
    for d in range(-DIFF_NEAR, DIFF_NEAR + 1):
        rel = d * t + (t - 1) - i
        u = jnp.transpose(table[_t5_bucket(rel)])
        skew = jnp.broadcast_to(u[:, None, :], (DIFF_HEADS, t, 2 * t)).reshape(DIFF_HEADS, 2 * t * t)
        skew = skew[:, :t * (2 * t - 1)].reshape(DIFF_HEADS, t, 2 * t - 1)
        tiles.append(skew[:, :, t - 1:])
    return jnp.stack(tiles, axis=1)


def _diff_kernel(far_ref, lam_ref, q_ref, k_ref, vt_ref, nb_ref, g_ref, o_ref, s_even, s_odd, acc_scr,
                 *, nk, post_scale):
    t = DIFF_T
    h = pl.program_id(0)
    qi = pl.program_id(2)
    q = q_ref[0]
    lane = lax.broadcasted_iota(jnp.int32, (t, HEAD_LANES), 1)
    first = lane < DIFF_HEAD_DIM
    zero = jnp.zeros_like(q)
    qs = (jnp.where(first, q, zero), jnp.where(first, zero, q))
    c_left = far_ref[h, 0]
    c_right = far_ref[h, 1]

    s_slots = (s_even, s_odd)

    def scores(j, slot, bias_tile, bias_const):
        k = k_ref[0, pl.ds(pl.multiple_of(j * t, t), t), :]
        cms = []
        for c in range(2):
            s = lax.dot_general(k, qs[c], _NT, preferred_element_type=F32)
            if bias_tile is not None:
                s = s + bias_tile
            s_slots[slot][c] = s
            cm = jnp.max(s, axis=0, keepdims=True)
            cms.append(cm if bias_const is None else cm + bias_const)
        return cms

    def consume(j, slot, cms, bias_const, stats):
        vt = vt_ref[j]
        new = []
        for c in range(2):
            m_old, l_old = stats[2 * c:2 * c + 2]
            m_new = jnp.maximum(m_old, cms[c])
            p = jnp.exp2(s_slots[slot][c] - (m_new - bias_const))
            alpha = jnp.exp2(m_old - m_new)
            l_new = alpha * l_old + jnp.sum(p, axis=0, keepdims=True)
            acc_scr[c] = alpha * acc_scr[c] + jnp.dot(vt, p.astype(BF16), preferred_element_type=F32)
            new += [m_new, l_new]
        return tuple(new)

    def step(j, slot, state, bias_tile, bias_const):
        cm1, cm2, c_prev, m1, l1, m2, l2 = state
        cms = scores(j, slot, bias_tile, bias_const)
        stats = consume(j - 1, 1 - slot, (cm1, cm2), c_prev, (m1, l1, m2, l2))
        c_new = jnp.float32(0.0) if bias_const is None else bias_const
        return (cms[0], cms[1], c_new) + stats

    def pair_loop(start, stop, state, near, bias_const):
        def body(i, st):
            j = start + 2 * i
            for jj, slot in ((j, 1), (j + 1, 0)):
                tile = nb_ref[0, jj - qi + DIFF_NEAR] if near else None
                st = step(jj, slot, st, tile, bias_const)
            return st
        return lax.fori_loop(0, (stop - start) // 2, body, state)

    m0 = jnp.full((1, t), NEG_INF, F32)
    l0 = jnp.zeros((1, t), F32)
    acc_scr[...] = jnp.zeros_like(acc_scr)
    last = nk - 1
    near_lo = qi - 1 - (qi % 2)
    near_hi = near_lo + 2 * DIFF_NEAR
    bias_first = jnp.where(near_lo <= 0, nb_ref[0, jnp.clip(DIFF_NEAR - qi, 0, 2 * DIFF_NEAR)], c_left)
    cm1, cm2 = scores(0, 0, bias_first, None)
    state = (cm1, cm2, jnp.float32(0.0), m0, l0, m0, l0)
    near_start = jnp.maximum(near_lo, 1)
    near_stop = jnp.minimum(near_hi, last)
    state = pair_loop(1, near_start, state, False, c_left)
    state = pair_loop(near_start, near_stop, state, True, None)
    state = pair_loop(near_stop, last, state, False, c_right)
    bias_last = jnp.where(near_hi > last,
                          nb_ref[0, jnp.clip(last - qi + DIFF_NEAR, 0, 2 * DIFF_NEAR)], c_right)
    state = step(last, 1, state, bias_last, None)
    cm1, cm2, c_prev, m1, l1, m2, l2 = state
    _, l1, _, l2 = consume(last, 1, (cm1, cm2), c_prev, (m1, l1, m2, l2))
    out = acc_scr[0] / l1 - lam_ref[0] * (acc_scr[1] / l2)
    out = out * lax.rsqrt(jnp.mean(out * out, axis=0, keepdims=True) + RMS_EPS) * g_ref[...]
    o_ref[0] = jnp.transpose(out * post_scale).astype(BF16)


def _diff_attention(qk3, vt, near_bias, far_bias, lam, subln_g, lambda_init):
    b, s, _ = qk3.shape
    t = DIFF_T
    nk = s // t
    assert nk >= 2 and nk % 2 == 0 and t > T5_MAX_DIST
    nh = DIFF_HEADS
    grid_spec = pltpu.PrefetchScalarGridSpec(
        num_scalar_prefetch=2,
        grid=(nh, b, nk),
        in_specs=[
            pl.BlockSpec((1, t, HEAD_LANES), lambda h, bb, i, *_: (bb, i, h)),
            pl.BlockSpec((1, s, HEAD_LANES), lambda h, bb, i, *_: (bb, 0, nh + h)),
            pl.BlockSpec((nk, HEAD_LANES, t), lambda h, bb, i, *_: (bb, h, 0)),
            pl.BlockSpec((1, 2 * DIFF_NEAR + 1, t, t), lambda h, bb, i, *_: (h, 0, 0, 0)),
            pl.BlockSpec((HEAD_LANES, 1), lambda h, bb, i, *_: (0, 0)),
        ],
        out_specs=pl.BlockSpec((1, t, HEAD_LANES), lambda h, bb, i, *_: (bb, i, h)),
        scratch_shapes=[
            pltpu.VMEM((2, t, t), F32),
            pltpu.VMEM((2, t, t), F32),
            pltpu.VMEM((2, HEAD_LANES, t), F32),
        ],
    )
    return pl.pallas_call(
        functools.partial(_diff_kernel, nk=nk, post_scale=1.0 - lambda_init),
        grid_spec=grid_spec,
        out_shape=jax.ShapeDtypeStruct((b, s, D_MODEL), BF16),
        compiler_params=pltpu.CompilerParams(
            dimension_semantics=("arbitrary", "arbitrary", "arbitrary"),
            vmem_limit_bytes=VMEM_LIMIT),
        name="diff_attention",
    )(far_bias, lam, qk3, qk3, vt, near_bias, subln_g.reshape(HEAD_LANES, 1))


def _scaled_q(w, head_dim):
    scale = head_dim ** -0.5
    assert math.frexp(scale)[0] == 0.5
    return w[:, :D_MODEL] * scale


def kernel(x_prompt, x_sample, mix_pre_g, mix_post_g, na_w_qkv, na_w_o, na_rpb, diff_w_qkv, diff_w_o,
           diff_lambda_q1, diff_lambda_k1, diff_lambda_q2, diff_lambda_k2, diff_subln_g, t5_bias,
           ffn_pre_g, ffn_post_g, ffn_w_gate, ffn_w_up, ffn_w_down):
    na_bias = [_na_bias_tiles(na_rpb[li]) for li in range(na_rpb.shape[0])]
    table = t5_bias.astype(F32) * LOG2E
    diff_near = _diff_bias_tiles(table)
    far = jnp.stack([table[T5_BUCKETS // 2 - 1], table[T5_BUCKETS - 1]], axis=1)
    wg = ffn_w_gate.astype(BF16)
    wu = ffn_w_up.astype(BF16)
    wd = ffn_w_down.astype(BF16)

    def trunk(x):
        b, s, d = x.shape
        x2 = x.reshape(b * s, d)
        for i in range(DEPTH):
            li = i // 2
            if i % 2 == 0:
                w = na_w_qkv[li]
                w_qkv = jnp.concatenate([_scaled_q(w, NA_HEAD_DIM), w[:, d:]], axis=1).astype(BF16)
                qkv = _qkv_proj(x2, mix_pre_g[i], w_qkv).reshape(b, s, 3 * d)
                att = _na_attention(qkv, na_bias[li])
                w_o = na_w_o[li]
            else:
                lambda_init = 0.8 - 0.6 * math.exp(-0.3 * i)
                lam = (jnp.exp(jnp.sum(diff_lambda_q1[li].astype(F32) * diff_lambda_k1[li].astype(F32)))
                       - jnp.exp(jnp.sum(diff_lambda_q2[li].astype(F32) * diff_lambda_k2[li].astype(F32)))
                       + lambda_init).reshape(1)
                w = diff_w_qkv[li]
                w_qk = jnp.concatenate([_scaled_q(w, DIFF_HEAD_DIM), w[:, d:2 * d]], axis=1).astype(BF16)
                w_vt = jnp.transpose(w[:, 2 * d:]).astype(BF16)
                qk, vt = _qk_vt_proj(x2, mix_pre_g[i], w_qk, w_vt)
                att = _diff_attention(qk.reshape(b, s, 2 * d), vt, diff_near, far, lam,
                                      diff_subln_g[li], lambda_init)
                w_o = diff_w_o[li]
            x2 = _post_ffn(att.reshape(b * s, d), x2, w_o.astype(BF16), mix_post_g[i],
                           ffn_pre_g[i], ffn_post_g[i], wg[i], wu[i], wd[i])
        return x2.reshape(b, s, d)

    return (trunk(x_prompt), trunk(x_sample))
```

```python
import functools
import math

import jax
import jax.numpy as jnp
from jax import lax
from jax.experimental import pallas as pl
from jax.experimental.pallas import tpu as pltpu

F32 = jnp.float32
BF16 = jnp.bfloat16

D_MODEL = 1024
DEPTH = 2
RMS_EPS = 1e-6
NEG_INF = -1e30

GRID_W = 64
NA_HEADS = 16
NA_HEAD_DIM = 64
NA_KH = 8
NA_KW = 16
NA_RQ = 4
NA_WK = 3 * NA_RQ
NA_TQ = NA_RQ * GRID_W
NA_TK = NA_WK * GRID_W
NA_G = 16

DIFF_HEADS = 8
DIFF_HEAD_DIM = 64
DIFF_T = 512
T5_BUCKETS = 32
T5_MAX_DIST = 128
DIFF_NEAR = 2

HEAD_LANES = 128
VT_GROUPS = D_MODEL // HEAD_LANES
VT_ROWS = HEAD_LANES + 16
D_FF = 2816

TM_QKV = 512
TM_FFN = 256
VMEM_LIMIT = 56 * 1024 * 1024

_NT = (((1,), (1,)), ((), ()))
LOG2E = math.log2(math.e)


def _rms(x, g):
    return x * lax.rsqrt(jnp.mean(x * x, axis=-1, keepdims=True) + RMS_EPS) * g


def _proj_kernel(x_ref, g_ref, wqt_ref, wk_ref, wvt_ref, qt_ref, k_ref, vt_ref, *, tile):
    h = _rms(x_ref[...], g_ref[...]).astype(BF16)
    k_ref[...] = jnp.dot(h, wk_ref[...], preferred_element_type=F32).astype(BF16)
    qt = (lax.dot_general(wqt_ref[...], h, _NT, preferred_element_type=F32) * LOG2E).astype(BF16)
    vt = lax.dot_general(wvt_ref[...], h, _NT, preferred_element_type=F32).astype(BF16)
    ones = jnp.ones((VT_ROWS - HEAD_LANES, tile), BF16)
    for i in range(TM_QKV // tile):
        cols = slice(i * tile, (i + 1) * tile)
        for grp in range(VT_GROUPS):
            rows = slice(grp * HEAD_LANES, (grp + 1) * HEAD_LANES)
            qt_ref[i, grp] = qt[rows, cols]
            vt_ref[i, grp, :HEAD_LANES, :] = vt[rows, cols]
            vt_ref[i, grp, HEAD_LANES:, :] = ones


def _proj(x2, g, wqt, wk, wvt, tile):
    n = x2.shape[0]
    per_step = TM_QKV // tile
    fixed = lambda i: (0, 0)
    return pl.pallas_call(
        functools.partial(_proj_kernel, tile=tile),
        grid=(n // TM_QKV,),
        in_specs=[
            pl.BlockSpec((TM_QKV, D_MODEL), lambda i: (i, 0)),
            pl.BlockSpec((1, D_MODEL), fixed),
            pl.BlockSpec((D_MODEL, D_MODEL), fixed),
            pl.BlockSpec((D_MODEL, D_MODEL), fixed),
            pl.BlockSpec((D_MODEL, D_MODEL), fixed),
        ],
        out_specs=[
            pl.BlockSpec((per_step, VT_GROUPS, HEAD_LANES, tile), lambda i: (i, 0, 0, 0)),
            pl.BlockSpec((TM_QKV, D_MODEL), lambda i: (i, 0)),
            pl.BlockSpec((per_step, VT_GROUPS, VT_ROWS, tile), lambda i: (i, 0, 0, 0)),
        ],
        out_shape=[
            jax.ShapeDtypeStruct((n // tile, VT_GROUPS, HEAD_LANES, tile), BF16),
            jax.ShapeDtypeStruct((n, D_MODEL), BF16),
            jax.ShapeDtypeStruct((n // tile, VT_GROUPS, VT_ROWS, tile), BF16),
        ],
        compiler_params=pltpu.CompilerParams(
            dimension_semantics=("arbitrary",), vmem_limit_bytes=VMEM_LIMIT),
        name="qkv_proj",
    )(x2, g.reshape(1, D_MODEL), wqt, wk, wvt)


def _post_ffn_kernel(a_ref, x_ref, wo_ref, gpost_ref, gfpre_ref, gfpost_ref,
                     wg_ref, wu_ref, wd_ref, o_ref):
    m = jnp.dot(a_ref[...], wo_ref[...], preferred_element_type=F32)
    x1 = x_ref[...] + _rms(m, gpost_ref[...])
    h = _rms(x1, gfpre_ref[...]).astype(BF16)
    gate = jnp.dot(h, wg_ref[...], preferred_element_type=F32)
    up = jnp.dot(h, wu_ref[...], preferred_element_type=F32)
    act = (gate * (1.0 / (1.0 + jnp.exp(-gate))) * up).astype(BF16)
    f = jnp.dot(act, wd_ref[...], preferred_element_type=F32)
    o_ref[...] = x1 + _rms(f, gfpost_ref[...])


def _post_ffn(att2, x2, wo, gpost, gfpre, gfpost, wg, wu, wd):
    n = x2.shape[0]
    row = lambda i: (i, 0)
    fixed = lambda i: (0, 0)
    once = pl.Buffered(1)
    vec = pl.BlockSpec((1, D_MODEL), fixed)
    return pl.pallas_call(
        _post_ffn_kernel,
        grid=(n // TM_FFN,),
        in_specs=[
            pl.BlockSpec((TM_FFN, D_MODEL), row),
            pl.BlockSpec((TM_FFN, D_MODEL), row),
            pl.BlockSpec((D_MODEL, D_MODEL), fixed, pipeline_mode=once),
            vec, vec, vec,
            pl.BlockSpec((D_MODEL, D_FF), fixed, pipeline_mode=once),
            pl.BlockSpec((D_MODEL, D_FF), fixed, pipeline_mode=once),
            pl.BlockSpec((D_FF, D_MODEL), fixed, pipeline_mode=once),
        ],
        out_specs=pl.BlockSpec((TM_FFN, D_MODEL), row),
        out_shape=jax.ShapeDtypeStruct((n, D_MODEL), F32),
        compiler_params=pltpu.CompilerParams(
            dimension_semantics=("arbitrary",), vmem_limit_bytes=VMEM_LIMIT),
        name="post_ffn",
    )(att2, x2, wo, gpost.reshape(1, D_MODEL), gfpre.reshape(1, D_MODEL),
      gfpost.reshape(1, D_MODEL), wg, wu, wd)


def _na_bias_tiles(rpb):
    rpb = rpb.astype(F32) * LOG2E
    a = jnp.arange(NA_RQ)
    w = jnp.arange(NA_WK)
    half = NA_KH // 2
    offs = (0, half, NA_WK - NA_RQ)
    los = (jnp.zeros_like(a), a, jnp.full_like(a, NA_WK - NA_KH))
    qc = jnp.arange(GRID_W)
    kc = jnp.arange(GRID_W)
    q_start = jnp.clip(qc - NA_KW // 2, 0, GRID_W - NA_KW)
    col_ok = (kc[:, None] >= q_start[None, :]) & (kc[:, None] < q_start[None, :] + NA_KW)
    dc = jnp.clip(kc[:, None] - qc[None, :] + (NA_KW - 1), 0, 2 * NA_KW - 2)
    by_col = rpb[:, :, dc]
    tiles = []
    for off, lo in zip(offs, los):
        row_ok = (w[:, None] >= lo[None, :]) & (w[:, None] < lo[None, :] + NA_KH)
        dr = jnp.clip(w[:, None] - (a[None, :] + off) + (NA_KH - 1), 0, 2 * NA_KH - 2)
        t = by_col[:, dr]
        ok = row_ok[:, :, None, None] & col_ok[None, None, :, :]
        t = jnp.where(ok[None], t, NEG_INF)
        t = jnp.transpose(t, (0, 1, 3, 2, 4)).reshape(NA_HEADS // 2, 2, NA_TK, NA_TQ)
        tiles.append(jnp.transpose(t, (0, 2, 1, 3)).reshape(NA_HEADS // 2, NA_TK, 2 * NA_TQ))
    return jnp.stack(tiles)


def _na_kernel(qt_ref, k_ref, vt_ref, bias_ref, o_ref, *, nblk, group):
    g = pl.program_id(2)
    first_rows = lax.broadcasted_iota(jnp.int32, (HEAD_LANES, NA_TQ), 0) < NA_HEAD_DIM
    ktiles = NA_TK // NA_TQ

    def scores(u):
        blk = g * group + u
        kt0 = jnp.clip(blk - 1, 0, nblk - ktiles)
        cls = jnp.where(blk == 0, 0, jnp.where(blk == nblk - 1, 2, 1))
        qt = qt_ref[u, 0]
        zero = jnp.zeros_like(qt)
        q2 = jnp.concatenate([jnp.where(first_rows, qt, zero), jnp.where(first_rows, zero, qt)], axis=1)
        k = k_ref[0, pl.ds(pl.multiple_of(kt0 * NA_TQ, NA_TQ), NA_TK), :]
        vt = jnp.concatenate([vt_ref[kt0 + i, 0] for i in range(ktiles)], axis=1)
        s = jnp.dot(k, q2, preferred_element_type=F32) + bias_ref[cls, 0]
        return s, vt

    def finish(u, s, vt):
        p = jnp.exp2(s - jnp.max(s, axis=0, keepdims=True)).astype(BF16)
        o = jnp.dot(vt, p, preferred_element_type=F32)
        l = o[HEAD_LANES:HEAD_LANES + 1, :]
        o = o[:HEAD_LANES, :]
        out_t = (jnp.where(first_rows, o[:, :NA_TQ], o[:, NA_TQ:])
                 / jnp.where(first_rows, l[:, :NA_TQ], l[:, NA_TQ:]))
        o_ref[0, u * NA_TQ:(u + 1) * NA_TQ, :] = jnp.transpose(out_t).astype(BF16)

    pending = scores(0)
    for u in range(1, group):
        ahead = scores(u)
        finish(u - 1, *pending)
        pending = ahead
    finish(group - 1, *pending)


def _na_attention(qt, k3, vt, bias):
    b, s, _ = k3.shape
    nblk = s // NA_TQ
    group = math.gcd(NA_G, nblk)
    steps = nblk // group
    assert nblk >= NA_TK // NA_TQ and s % NA_TQ == 0
    npair = NA_HEADS // 2
    return pl.pallas_call(
        functools.partial(_na_kernel, nblk=nblk, group=group),
        grid=(npair, b, steps),
        in_specs=[
            pl.BlockSpec((group, 1, HEAD_LANES, NA_TQ), lambda hp, bb, i: (bb * steps + i, hp, 0, 0)),
            pl.BlockSpec((1, s, HEAD_LANES), lambda hp, bb, i: (bb, 0, hp)),
            pl.BlockSpec((nblk, 1, VT_ROWS, NA_TQ), lambda hp, bb, i: (bb, hp, 0, 0)),
            pl.BlockSpec((3, 1, NA_TK, 2 * NA_TQ), lambda hp, bb, i: (0, hp, 0, 0)),
        ],
        out_specs=pl.BlockSpec((1, group * NA_TQ, HEAD_LANES), lambda hp, bb, i: (bb, i, hp)),
        out_shape=jax.ShapeDtypeStruct((b, s, D_MODEL), BF16),
        compiler_params=pltpu.CompilerParams(
            dimension_semantics=("arbitrary", "arbitrary", "arbitrary"),
            vmem_limit_bytes=VMEM_LIMIT),
        name="na_attention",
    )(qt, k3, vt, bias)


def _t5_bucket(rel):
    nb = T5_BUCKETS // 2
    max_exact = nb // 2
    ret = jnp.where(rel > 0, nb, 0)
    n = jnp.abs(rel)
    nf = jnp.maximum(n, max_exact).astype(F32)
    large = max_exact + (jnp.log(nf / max_exact) / math.log(T5_MAX_DIST / max_exact)
                         * (nb - max_exact)).astype(jnp.int32)
    large = jnp.minimum(large, nb - 1)
    return ret + jnp.where(n < max_exact, n, large)


def _diff_bias_tiles(t5_bias):
    table = t5_bias.astype(F32)
    t = DIFF_T
    i = jnp.arange(2 * t)
    tiles = []
    for d in range(-DIFF_NEAR, DIFF_NEAR + 1):
        rel = d * t + (t - 1) - i
        u = jnp.transpose(table[_t5_bucket(rel)])
        skew = jnp.broadcast_to(u[:, None, :], (DIFF_HEADS, t, 2 * t)).reshape(DIFF_HEADS, 2 * t * t)
        skew = skew[:, :t * (2 * t - 1)].reshape(DIFF_HEADS, t, 2 * t - 1)
        tiles.append(skew[:, :, t - 1:])
    return jnp.stack(tiles, axis=1)


def _diff_kernel(far_ref, lam_ref, qt_ref, k_ref, vt_ref, nb_ref, g_ref, o_ref, s_even, s_odd, acc_scr,
                 *, nk, post_scale):
    t = DIFF_T
    h = pl.program_id(0)
    qi = pl.program_id(2)
    qt = qt_ref[0, 0]
    first = lax.broadcasted_iota(jnp.int32, (HEAD_LANES, t), 0) < DIFF_HEAD_DIM
    zero = jnp.zeros_like(qt)
    qs = (jnp.where(first, qt, zero), jnp.where(first, zero, qt))
    c_left = far_ref[h, 0]
    c_right = far_ref[h, 1]

    s_slots = (s_even, s_odd)

    def scores(j, slot, bias_tile, bias_const):
        k = k_ref[0, pl.ds(pl.multiple_of(j * t, t), t), :]
        cms = []
        for c in range(2):
            s = jnp.dot(k, qs[c], preferred_element_type=F32)
            if bias_tile is not None:
                s = s + bias_tile
            s_slots[slot][c] = s
            cm = jnp.max(s, axis=0, keepdims=True)
            cms.append(cm if bias_const is None else cm + bias_const)
        return cms

    def consume(j, slot, cms, bias_const, stats):
        vt = vt_ref[j, 0]
        new = []
        for c in range(2):
            m_old = stats[c]
            m_new = jnp.maximum(m_old, cms[c])
            p = jnp.exp2(s_slots[slot][c] - (m_new - bias_const)).astype(BF16)
            alpha = jnp.exp2(m_old - m_new)
            acc_scr[c] = alpha * acc_scr[c] + jnp.dot(vt, p, preferred_element_type=F32)
            new.append(m_new)
        return tuple(new)

    def step(j, slot, state, bias_tile, bias_const):
        cm1, cm2, c_prev, m1, m2 = state
        cms = scores(j, slot, bias_tile, bias_const)
        stats = consume(j - 1, 1 - slot, (cm1, cm2), c_prev, (m1, m2))
        c_new = jnp.float32(0.0) if bias_const is None else bias_const
        return (cms[0], cms[1], c_new) + stats

    def pair_loop(start, stop, state, near, bias_const):
        def body(i, st):
            j = start + 2 * i
            for jj, slot in ((j, 1), (j + 1, 0)):
                tile = nb_ref[0, jj - qi + DIFF_NEAR] if near else None
                st = step(jj, slot, st, tile, bias_const)
            return st
        return lax.fori_loop(0, (stop - start) // 2, body, state)

    m0 = jnp.full((1, t), NEG_INF, F32)
    acc_scr[...] = jnp.zeros_like(acc_scr)
    last = nk - 1
    near_lo = qi - 1 - (qi % 2)
    near_hi = near_lo + 2 * DIFF_NEAR
    bias_first = jnp.where(near_lo <= 0, nb_ref[0, jnp.clip(DIFF_NEAR - qi, 0, 2 * DIFF_NEAR)], c_left)
    cm1, cm2 = scores(0, 0, bias_first, None)
    state = (cm1, cm2, jnp.float32(0.0), m0, m0)
    near_start = jnp.maximum(near_lo, 1)
    near_stop = jnp.minimum(near_hi, last)
    state = pair_loop(1, near_start, state, False, c_left)
    state = pair_loop(near_start, near_stop, state, True, None)
    state = pair_loop(near_stop, last, state, False, c_right)
    bias_last = jnp.where(near_hi > last,
                          nb_ref[0, jnp.clip(last - qi + DIFF_NEAR, 0, 2 * DIFF_NEAR)], c_right)
    state = step(last, 1, state, bias_last, None)
    cm1, cm2, c_prev, m1, m2 = state
    consume(last, 1, (cm1, cm2), c_prev, (m1, m2))
    num = [acc_scr[c, :HEAD_LANES, :] for c in range(2)]
    den = [acc_scr[c, HEAD_LANES:HEAD_LANES + 1, :] for c in range(2)]
    out = num[0] / den[0] - lam_ref[0] * (num[1] / den[1])
    out = out * lax.rsqrt(jnp.mean(out * out, axis=0, keepdims=True) + RMS_EPS) * g_ref[...]
    o_ref[0] = jnp.transpose(out * post_scale).astype(BF16)


def _diff_attention(qt, k3, vt, near_bias, far_bias, lam, subln_g, lambda_init):
    b, s, _ = k3.shape
    t = DIFF_T
    nk = s // t
    assert nk >= 2 and nk % 2 == 0 and t > T5_MAX_DIST
    nh = DIFF_HEADS
    grid_spec = pltpu.PrefetchScalarGridSpec(
        num_scalar_prefetch=2,
        grid=(nh, b, nk),
        in_specs=[
            pl.BlockSpec((1, 1, HEAD_LANES, t), lambda h, bb, i, *_: (bb * nk + i, h, 0, 0)),
            pl.BlockSpec((1, s, HEAD_LANES), lambda h, bb, i, *_: (bb, 0, h)),
            pl.BlockSpec((nk, 1, VT_ROWS, t), lambda h, bb, i, *_: (bb, h, 0, 0)),
            pl.BlockSpec((1, 2 * DIFF_NEAR + 1, t, t), lambda h, bb, i, *_: (h, 0, 0, 0)),
            pl.BlockSpec((HEAD_LANES, 1), lambda h, bb, i, *_: (0, 0)),
        ],
        out_specs=pl.BlockSpec((1, t, HEAD_LANES), lambda h, bb, i, *_: (bb, i, h)),
        scratch_shapes=[
            pltpu.VMEM((2, t, t), F32),
            pltpu.VMEM((2, t, t), F32),
            pltpu.VMEM((2, VT_ROWS, t), F32),
        ],
    )
    return pl.pallas_call(
        functools.partial(_diff_kernel, nk=nk, post_scale=1.0 - lambda_init),
        grid_spec=grid_spec,
        out_shape=jax.ShapeDtypeStruct((b, s, D_MODEL), BF16),
        compiler_params=pltpu.CompilerParams(
            dimension_semantics=("arbitrary", "arbitrary", "arbitrary"),
            vmem_limit_bytes=VMEM_LIMIT),
        name="diff_attention",
    )(far_bias, lam, qt, k3, vt, near_bias, subln_g.reshape(HEAD_LANES, 1))


def _split_qkv(w, head_dim):
    scale = head_dim ** -0.5
    assert math.frexp(scale)[0] == 0.5
    d = D_MODEL
    return (jnp.transpose(w[:, :d] * scale).astype(BF16), w[:, d:2 * d].astype(BF16),
            jnp.transpose(w[:, 2 * d:]).astype(BF16))


def kernel(x_prompt, x_sample, mix_pre_g, mix_post_g, na_w_qkv, na_w_o, na_rpb, diff_w_qkv, diff_w_o,
           diff_lambda_q1, diff_lambda_k1, diff_lambda_q2, diff_lambda_k2, diff_subln_g, t5_bias,
           ffn_pre_g, ffn_post_g, ffn_w_gate, ffn_w_up, ffn_w_down):
    na_bias = [_na_bias_tiles(na_rpb[li]) for li in range(na_rpb.shape[0])]
    table = t5_bias.astype(F32) * LOG2E
    diff_near = _diff_bias_tiles(table)
    far = jnp.stack([table[T5_BUCKETS // 2 - 1], table[T5_BUCKETS - 1]], axis=1)
    wg = ffn_w_gate.astype(BF16)
    wu = ffn_w_up.astype(BF16)
    wd = ffn_w_down.astype(BF16)

    def trunk(x):
        b, s, d = x.shape
        x2 = x.reshape(b * s, d)
        for i in range(DEPTH):
            li = i // 2
            if i % 2 == 0:
                qt, k, vt = _proj(x2, mix_pre_g[i], *_split_qkv(na_w_qkv[li], NA_HEAD_DIM), NA_TQ)
                att = _na_attention(qt, k.reshape(b, s, d), vt, na_bias[li])
                w_o = na_w_o[li]
            else:
                lambda_init = 0.8 - 0.6 * math.exp(-0.3 * i)
                lam = (jnp.exp(jnp.sum(diff_lambda_q1[li].astype(F32) * diff_lambda_k1[li].astype(F32)))
                       - jnp.exp(jnp.sum(diff_lambda_q2[li].astype(F32) * diff_lambda_k2[li].astype(F32)))
                       + lambda_init).reshape(1)
                qt, k, vt = _proj(x2, mix_pre_g[i], *_split_qkv(diff_w_qkv[li], DIFF_HEAD_DIM), DIFF_T)
                att = _diff_attention(qt, k.reshape(b, s, d), vt, diff_near, far, lam,
                                      diff_subln_g[li], lambda_init)
                w_o = diff_w_o[li]
            x2 = _post_ffn(att.reshape(b * s, d), x2, w_o.astype(BF16), mix_post_g[i],
                           ffn_pre_g[i], ffn_post_g[i], wg[i], wu[i], wd[i])
        return x2.reshape(b, s, d)

    return (trunk(x_prompt), trunk(x_sample))
```

```python
import functools
import math

import jax
import jax.numpy as jnp
from jax import lax
from jax.experimental import pallas as pl
from jax.experimental.pallas import tpu as pltpu

F32 = jnp.float32
BF16 = jnp.bfloat16

D_MODEL = 1024
DEPTH = 2
RMS_EPS = 1e-6
NEG_INF = -1e30

GRID_W = 64
NA_HEADS = 16
NA_HEAD_DIM = 64
NA_KH = 8
NA_KW = 16
NA_RQ = 4
NA_WK = 3 * NA_RQ
NA_TQ = NA_RQ * GRID_W
NA_TK = NA_WK * GRID_W
NA_G = 16

DIFF_HEADS = 8
DIFF_HEAD_DIM = 64
DIFF_T = 512
T5_BUCKETS = 32
T5_MAX_DIST = 128
DIFF_NEAR = 2
DIFF_UNROLL = 4

HEAD_LANES = 128
VT_GROUPS = D_MODEL // HEAD_LANES
VT_ROWS = HEAD_LANES + 16
D_FF = 2816

TM_QKV = 512
TM_FFN = 512
VMEM_LIMIT = 56 * 1024 * 1024

_NT = (((1,), (1,)), ((), ()))
LOG2E = math.log2(math.e)


def _rms(x, g):
    return x * lax.rsqrt(jnp.mean(x * x, axis=-1, keepdims=True) + RMS_EPS) * g


def _proj_kernel(x_ref, g_ref, wqt_ref, wk_ref, wvt_ref, qt_ref, k_ref, vt_ref, *, tile):
    h = _rms(x_ref[...], g_ref[...]).astype(BF16)
    k_ref[...] = jnp.dot(h, wk_ref[...], preferred_element_type=F32).astype(BF16)
    qt = (lax.dot_general(wqt_ref[...], h, _NT, preferred_element_type=F32) * LOG2E).astype(BF16)
    vt = lax.dot_general(wvt_ref[...], h, _NT, preferred_element_type=F32).astype(BF16)
    ones = jnp.ones((VT_ROWS - HEAD_LANES, tile), BF16)
    for i in range(TM_QKV // tile):
        cols = slice(i * tile, (i + 1) * tile)
        for grp in range(VT_GROUPS):
            rows = slice(grp * HEAD_LANES, (grp + 1) * HEAD_LANES)
            qt_ref[i, grp] = qt[rows, cols]
            vt_ref[i, grp, :HEAD_LANES, :] = vt[rows, cols]
            vt_ref[i, grp, HEAD_LANES:, :] = ones


def _proj(x2, g, wqt, wk, wvt, tile):
    n = x2.shape[0]
    per_step = TM_QKV // tile
    fixed = lambda i: (0, 0)
    return pl.pallas_call(
        functools.partial(_proj_kernel, tile=tile),
        grid=(n // TM_QKV,),
        in_specs=[
            pl.BlockSpec((TM_QKV, D_MODEL), lambda i: (i, 0)),
            pl.BlockSpec((1, D_MODEL), fixed),
            pl.BlockSpec((D_MODEL, D_MODEL), fixed),
            pl.BlockSpec((D_MODEL, D_MODEL), fixed),
            pl.BlockSpec((D_MODEL, D_MODEL), fixed),
        ],
        out_specs=[
            pl.BlockSpec((per_step, VT_GROUPS, HEAD_LANES, tile), lambda i: (i, 0, 0, 0)),
            pl.BlockSpec((TM_QKV, D_MODEL), lambda i: (i, 0)),
            pl.BlockSpec((per_step, VT_GROUPS, VT_ROWS, tile), lambda i: (i, 0, 0, 0)),
        ],
        out_shape=[
            jax.ShapeDtypeStruct((n // tile, VT_GROUPS, HEAD_LANES, tile), BF16),
            jax.ShapeDtypeStruct((n, D_MODEL), BF16),
            jax.ShapeDtypeStruct((n // tile, VT_GROUPS, VT_ROWS, tile), BF16),
        ],
        compiler_params=pltpu.CompilerParams(
            dimension_semantics=("arbitrary",), vmem_limit_bytes=VMEM_LIMIT),
        name="qkv_proj",
    )(x2, g.reshape(1, D_MODEL), wqt, wk, wvt)


def _post_ffn_kernel(a_ref, x_ref, wo_ref, gpost_ref, gfpre_ref, gfpost_ref,
                     wg_ref, wu_ref, wd_ref, o_ref):
    m = jnp.dot(a_ref[...], wo_ref[...], preferred_element_type=F32)
    x1 = x_ref[...] + _rms(m, gpost_ref[...])
    h = _rms(x1, gfpre_ref[...]).astype(BF16)
    gate = jnp.dot(h, wg_ref[...], preferred_element_type=F32)
    up = jnp.dot(h, wu_ref[...], preferred_element_type=F32)
    act = (gate * (1.0 / (1.0 + jnp.exp(-gate))) * up).astype(BF16)
    f = jnp.dot(act, wd_ref[...], preferred_element_type=F32)
    o_ref[...] = x1 + _rms(f, gfpost_ref[...])


def _post_ffn(att2, x2, wo, gpost, gfpre, gfpost, wg, wu, wd):
    n = x2.shape[0]
    row = lambda i: (i, 0)
    fixed = lambda i: (0, 0)
    once = pl.Buffered(1)
    vec = pl.BlockSpec((1, D_MODEL), fixed)
    return pl.pallas_call(
        _post_ffn_kernel,
        grid=(n // TM_FFN,),
        in_specs=[
            pl.BlockSpec((TM_FFN, D_MODEL), row),
            pl.BlockSpec((TM_FFN, D_MODEL), row),
            pl.BlockSpec((D_MODEL, D_MODEL), fixed, pipeline_mode=once),
            vec, vec, vec,
            pl.BlockSpec((D_MODEL, D_FF), fixed, pipeline_mode=once),
            pl.BlockSpec((D_MODEL, D_FF), fixed, pipeline_mode=once),
            pl.BlockSpec((D_FF, D_MODEL), fixed, pipeline_mode=once),
        ],
        out_specs=pl.BlockSpec((TM_FFN, D_MODEL), row),
        out_shape=jax.ShapeDtypeStruct((n, D_MODEL), F32),
        compiler_params=pltpu.CompilerParams(
            dimension_semantics=("arbitrary",), vmem_limit_bytes=VMEM_LIMIT),
        name="post_ffn",
    )(att2, x2, wo, gpost.reshape(1, D_MODEL), gfpre.reshape(1, D_MODEL),
      gfpost.reshape(1, D_MODEL), wg, wu, wd)


def _na_bias_tiles(rpb):
    rpb = rpb.astype(F32) * LOG2E
    a = jnp.arange(NA_RQ)
    w = jnp.arange(NA_WK)
    half = NA_KH // 2
    offs = (0, half, NA_WK - NA_RQ)
    los = (jnp.zeros_like(a), a, jnp.full_like(a, NA_WK - NA_KH))
    qc = jnp.arange(GRID_W)
    kc = jnp.arange(GRID_W)
    q_start = jnp.clip(qc - NA_KW // 2, 0, GRID_W - NA_KW)
    col_ok = (kc[:, None] >= q_start[None, :]) & (kc[:, None] < q_start[None, :] + NA_KW)
    dc = jnp.clip(kc[:, None] - qc[None, :] + (NA_KW - 1), 0, 2 * NA_KW - 2)
    by_col = rpb[:, :, dc]
    tiles = []
    for off, lo in zip(offs, los):
        row_ok = (w[:, None] >= lo[None, :]) & (w[:, None] < lo[None, :] + NA_KH)
        dr = jnp.clip(w[:, None] - (a[None, :] + off) + (NA_KH - 1), 0, 2 * NA_KH - 2)
        t = by_col[:, dr]
        ok = row_ok[:, :, None, None] & col_ok[None, None, :, :]
        t = jnp.where(ok[None], t, NEG_INF)
        t = jnp.transpose(t, (0, 1, 3, 2, 4)).reshape(NA_HEADS // 2, 2, NA_TK, NA_TQ)
        tiles.append(jnp.transpose(t, (0, 2, 1, 3)).reshape(NA_HEADS // 2, NA_TK, 2 * NA_TQ))
    return jnp.stack(tiles)


def _na_kernel(qt_ref, k_ref, vt_ref, bias_ref, o_ref, *, nblk, group):
    g = pl.program_id(2)
    first_rows = lax.broadcasted_iota(jnp.int32, (HEAD_LANES, NA_TQ), 0) < NA_HEAD_DIM
    ktiles = NA_TK // NA_TQ

    def scores(u):
        blk = g * group + u
        kt0 = jnp.clip(blk - 1, 0, nblk - ktiles)
        cls = jnp.where(blk == 0, 0, jnp.where(blk == nblk - 1, 2, 1))
        qt = qt_ref[u, 0]
        zero = jnp.zeros_like(qt)
        q2 = jnp.concatenate([jnp.where(first_rows, qt, zero), jnp.where(first_rows, zero, qt)], axis=1)
        k = k_ref[0, pl.ds(pl.multiple_of(kt0 * NA_TQ, NA_TQ), NA_TK), :]
        vt = jnp.concatenate([vt_ref[kt0 + i, 0] for i in range(ktiles)], axis=1)
        s = jnp.dot(k, q2, preferred_element_type=F32) + bias_ref[cls, 0]
        return s, vt

    def finish(u, s, vt):
        p = jnp.exp2(s - jnp.max(s, axis=0, keepdims=True)).astype(BF16)
        o = jnp.dot(vt, p, preferred_element_type=F32)
        l = o[HEAD_LANES:HEAD_LANES + 1, :]
        o = o[:HEAD_LANES, :]
        out_t = (jnp.where(first_rows, o[:, :NA_TQ], o[:, NA_TQ:])
                 / jnp.where(first_rows, l[:, :NA_TQ], l[:, NA_TQ:]))
        o_ref[0, u * NA_TQ:(u + 1) * NA_TQ, :] = jnp.transpose(out_t).astype(BF16)

    pending = scores(0)
    for u in range(1, group):
        ahead = scores(u)
        finish(u - 1, *pending)
        pending = ahead
    finish(group - 1, *pending)


def _na_attention(qt, k3, vt, bias):
    b, s, _ = k3.shape
    nblk = s // NA_TQ
    group = math.gcd(NA_G, nblk)
    steps = nblk // group
    assert nblk >= NA_TK // NA_TQ and s % NA_TQ == 0
    npair = NA_HEADS // 2
    return pl.pallas_call(
        functools.partial(_na_kernel, nblk=nblk, group=group),
        grid=(npair, b, steps),
        in_specs=[
            pl.BlockSpec((group, 1, HEAD_LANES, NA_TQ), lambda hp, bb, i: (bb * steps + i, hp, 0, 0)),
            pl.BlockSpec((1, s, HEAD_LANES), lambda hp, bb, i: (bb, 0, hp)),
            pl.BlockSpec((nblk, 1, VT_ROWS, NA_TQ), lambda hp, bb, i: (bb, hp, 0, 0)),
            pl.BlockSpec((3, 1, NA_TK, 2 * NA_TQ), lambda hp, bb, i: (0, hp, 0, 0)),
        ],
        out_specs=pl.BlockSpec((1, group * NA_TQ, HEAD_LANES), lambda hp, bb, i: (bb, i, hp)),
        out_shape=jax.ShapeDtypeStruct((b, s, D_MODEL), BF16),
        compiler_params=pltpu.CompilerParams(
            dimension_semantics=("arbitrary", "arbitrary", "arbitrary"),
            vmem_limit_bytes=VMEM_LIMIT),
        name="na_attention",
    )(qt, k3, vt, bias)


def _t5_bucket(rel):
    nb = T5_BUCKETS // 2
    max_exact = nb // 2
    ret = jnp.where(rel > 0, nb, 0)
    n = jnp.abs(rel)
    nf = jnp.maximum(n, max_exact).astype(F32)
    large = max_exact + (jnp.log(nf / max_exact) / math.log(T5_MAX_DIST / max_exact)
                         * (nb - max_exact)).astype(jnp.int32)
    large = jnp.minimum(large, nb - 1)
    return ret + jnp.where(n < max_exact, n, large)


def _diff_bias_tiles(t5_bias):
    table = t5_bias.astype(F32)
    t = DIFF_T
    i = jnp.arange(2 * t)
    tiles = []
    for d in range(-DIFF_NEAR, DIFF_NEAR + 1):
        rel = d * t + (t - 1) - i
        u = jnp.transpose(table[_t5_bucket(rel)])
        skew = jnp.broadcast_to(u[:, None, :], (DIFF_HEADS, t, 2 * t)).reshape(DIFF_HEADS, 2 * t * t)
        skew = skew[:, :t * (2 * t - 1)].reshape(DIFF_HEADS, t, 2 * t - 1)
        tiles.append(skew[:, :, t - 1:])
    return jnp.stack(tiles, axis=1)


def _diff_kernel(far_ref, lam_ref, qt_ref, k_ref, vt_ref, nb_ref, g_ref, o_ref, s_even, s_odd, acc_scr,
                 *, nk, post_scale):
    t = DIFF_T
    h = pl.program_id(0)
    qi = pl.program_id(2)
    qt = qt_ref[0, 0]
    first = lax.broadcasted_iota(jnp.int32, (HEAD_LANES, t), 0) < DIFF_HEAD_DIM
    zero = jnp.zeros_like(qt)
    qs = (jnp.where(first, qt, zero), jnp.where(first, zero, qt))
    c_left = far_ref[h, 0]
    c_right = far_ref[h, 1]

    s_slots = (s_even, s_odd)

    def scores(j, slot, c, bias_tile, bias_const):
        k = k_ref[0, pl.ds(pl.multiple_of(j * t, t), t), :]
        s = jnp.dot(k, qs[c], preferred_element_type=F32)
        if bias_tile is not None:
            s = s + bias_tile
        s_slots[slot][c] = s
        cm = jnp.max(s, axis=0, keepdims=True)
        return cm if bias_const is None else cm + bias_const

    def consume(j, slot, c, cm, bias_const, m_old):
        vt = vt_ref[j, 0]
        m_new = jnp.maximum(m_old, cm)
        p = jnp.exp2(s_slots[slot][c] - (m_new - bias_const)).astype(BF16)
        alpha = jnp.exp2(m_old - m_new)
        acc_scr[c] = alpha * acc_scr[c] + jnp.dot(vt, p, preferred_element_type=F32)
        return m_new

    def step(j, slot, state, bias_tile, bias_const):
        cm_prev, c_prev, m_prev = state
        cms, ms = [], []
        for c in range(2):
            cms.append(scores(j, slot, c, bias_tile, bias_const))
            ms.append(consume(j - 1, 1 - slot, c, cm_prev[c], c_prev, m_prev[c]))
        c_new = jnp.float32(0.0) if bias_const is None else bias_const
        return (tuple(cms), c_new, tuple(ms))

    def tile_loop(start, stop, state, near, bias_const):
        def run(first, trips, width, st0):
            def body(i, st):
                j = first + width * i
                for d in range(width):
                    tile = nb_ref[0, j + d - qi + DIFF_NEAR] if near else None
                    st = step(j + d, (1 + d) % 2, st, tile, bias_const)
                return st
            return lax.fori_loop(0, trips, body, st0)
        wide = (stop - start) // DIFF_UNROLL
        state = run(start, wide, DIFF_UNROLL, state)
        rest = start + DIFF_UNROLL * wide
        return run(rest, (stop - rest) // 2, 2, state)

    m0 = jnp.full((1, t), NEG_INF, F32)
    acc_scr[...] = jnp.zeros_like(acc_scr)
    last = nk - 1
    near_lo = qi - 1 - (qi % 2)
    near_hi = near_lo + 2 * DIFF_NEAR
    bias_first = jnp.where(near_lo <= 0, nb_ref[0, jnp.clip(DIFF_NEAR - qi, 0, 2 * DIFF_NEAR)], c_left)
    state = (tuple(scores(0, 0, c, bias_first, None) for c in range(2)), jnp.float32(0.0), (m0, m0))
    near_start = jnp.maximum(near_lo, 1)
    near_stop = jnp.minimum(near_hi, last)
    state = tile_loop(1, near_start, state, False, c_left)
    state = tile_loop(near_start, near_stop, state, True, None)
    state = tile_loop(near_stop, last, state, False, c_right)
    bias_last = jnp.where(near_hi > last,
                          nb_ref[0, jnp.clip(last - qi + DIFF_NEAR, 0, 2 * DIFF_NEAR)], c_right)
    state = step(last, 1, state, bias_last, None)
    cm_prev, c_prev, m_prev = state
    for c in range(2):
        consume(last, 1, c, cm_prev[c], c_prev, m_prev[c])
    num = [acc_scr[c, :HEAD_LANES, :] for c in range(2)]
    den = [acc_scr[c, HEAD_LANES:HEAD_LANES + 1, :] for c in range(2)]
    out = num[0] / den[0] - lam_ref[0] * (num[1] / den[1])
    out = out * lax.rsqrt(jnp.mean(out * out, axis=0, keepdims=True) + RMS_EPS) * g_ref[...]
    o_ref[0] = jnp.transpose(out * post_scale).astype(BF16)


def _diff_attention(qt, k3, vt, near_bias, far_bias, lam, subln_g, lambda_init):
    b, s, _ = k3.shape
    t = DIFF_T
    nk = s // t
    assert nk >= 2 and nk % 2 == 0 and t > T5_MAX_DIST
    nh = DIFF_HEADS
    grid_spec = pltpu.PrefetchScalarGridSpec(
        num_scalar_prefetch=2,
        grid=(nh, b, nk),
        in_specs=[
            pl.BlockSpec((1, 1, HEAD_LANES, t), lambda h, bb, i, *_: (bb * nk + i, h, 0, 0)),
            pl.BlockSpec((1, s, HEAD_LANES), lambda h, bb, i, *_: (bb, 0, h)),
            pl.BlockSpec((nk, 1, VT_ROWS, t), lambda h, bb, i, *_: (bb, h, 0, 0)),
            pl.BlockSpec((1, 2 * DIFF_NEAR + 1, t, t), lambda h, bb, i, *_: (h, 0, 0, 0)),
            pl.BlockSpec((HEAD_LANES, 1), lambda h, bb, i, *_: (0, 0)),
        ],
        out_specs=pl.BlockSpec((1, t, HEAD_LANES), lambda h, bb, i, *_: (bb, i, h)),
        scratch_shapes=[
            pltpu.VMEM((2, t, t), F32),
            pltpu.VMEM((2, t, t), F32),
            pltpu.VMEM((2, VT_ROWS, t), F32),
        ],
    )
    return pl.pallas_call(
        functools.partial(_diff_kernel, nk=nk, post_scale=1.0 - lambda_init),
        grid_spec=grid_spec,
        out_shape=jax.ShapeDtypeStruct((b, s, D_MODEL), BF16),
        compiler_params=pltpu.CompilerParams(
            dimension_semantics=("arbitrary", "arbitrary", "arbitrary"),
            vmem_limit_bytes=VMEM_LIMIT),
        name="diff_attention",
    )(far_bias, lam, qt, k3, vt, near_bias, subln_g.reshape(HEAD_LANES, 1))


def _split_qkv(w, head_dim):
    scale = head_dim ** -0.5
    assert math.frexp(scale)[0] == 0.5
    d = D_MODEL
    return (jnp.transpose(w[:, :d] * scale).astype(BF16), w[:, d:2 * d].astype(BF16),
            jnp.transpose(w[:, 2 * d:]).astype(BF16))


def kernel(x_prompt, x_sample, mix_pre_g, mix_post_g, na_w_qkv, na_w_o, na_rpb, diff_w_qkv, diff_w_o,
           diff_lambda_q1, diff_lambda_k1, diff_lambda_q2, diff_lambda_k2, diff_subln_g, t5_bias,
           ffn_pre_g, ffn_post_g, ffn_w_gate, ffn_w_up, ffn_w_down):
    na_bias = [_na_bias_tiles(na_rpb[li]) for li in range(na_rpb.shape[0])]
    table = t5_bias.astype(F32) * LOG2E
    diff_near = _diff_bias_tiles(table)
    far = jnp.stack([table[T5_BUCKETS // 2 - 1], table[T5_BUCKETS - 1]], axis=1)
    wg = ffn_w_gate.astype(BF16)
    wu = ffn_w_up.astype(BF16)
    wd = ffn_w_down.astype(BF16)

    def trunk(x):
        b, s, d = x.shape
        x2 = x.reshape(b * s, d)
        for i in range(DEPTH):
            li = i // 2
            if i % 2 == 0:
                qt, k, vt = _proj(x2, mix_pre_g[i], *_split_qkv(na_w_qkv[li], NA_HEAD_DIM), NA_TQ)
                att = _na_attention(qt, k.reshape(b, s, d), vt, na_bias[li])
                w_o = na_w_o[li]
            else:
                lambda_init = 0.8 - 0.6 * math.exp(-0.3 * i)
                lam = (jnp.exp(jnp.sum(diff_lambda_q1[li].astype(F32) * diff_lambda_k1[li].astype(F32)))
                       - jnp.exp(jnp.sum(diff_lambda_q2[li].astype(F32) * diff_lambda_k2[li].astype(F32)))
                       + lambda_init).reshape(1)
                qt, k, vt = _proj(x2, mix_pre_g[i], *_split_qkv(diff_w_qkv[li], DIFF_HEAD_DIM), DIFF_T)
                att = _diff_attention(qt, k.reshape(b, s, d), vt, diff_near, far, lam,
                                      diff_subln_g[li], lambda_init)
                w_o = diff_w_o[li]
            x2 = _post_ffn(att.reshape(b * s, d), x2, w_o.astype(BF16), mix_post_g[i],
                           ffn_pre_g[i], ffn_post_g[i], wg[i], wu[i], wd[i])
        return x2.reshape(b, s, d)

    return (trunk(x_prompt), trunk(x_sample))
```

```python
import functools
import math

import jax
import jax.numpy as jnp
import numpy as np
from jax import lax
from jax.experimental import pallas as pl
from jax.experimental.pallas import tpu as pltpu

F32 = jnp.float32
BF16 = jnp.bfloat16

D_MODEL = 1024
DEPTH = 2
RMS_EPS = 1e-6
NEG_INF = -1e30

GRID_W = 64
NA_HEADS = 16
NA_HEAD_DIM = 64
NA_KH = 8
NA_KW = 16
NA_RQ = 4
NA_WK = 3 * NA_RQ
NA_TQ = NA_RQ * GRID_W
NA_TK = NA_WK * GRID_W
NA_G = 16

DIFF_HEADS = 8
DIFF_HEAD_DIM = 64
DIFF_T = 512
T5_BUCKETS = 32
T5_MAX_DIST = 128
DIFF_NEAR = 2
DIFF_UNROLL = 8
DIFF_QSPLIT = 2

HEAD_LANES = 128
VT_GROUPS = D_MODEL // HEAD_LANES
VT_ROWS = HEAD_LANES + 16
D_FF = 2816

TM_QKV = 512
TM_FFN = 512
VMEM_LIMIT = 56 * 1024 * 1024

_NT = (((1,), (1,)), ((), ()))
LOG2E = math.log2(math.e)


def _rms(x, g):
    return x * lax.rsqrt(jnp.mean(x * x, axis=-1, keepdims=True) + RMS_EPS) * g


def _proj_kernel(x_ref, g_ref, wqt_ref, wk_ref, wvt_ref, qt_ref, k_ref, vt_ref, *, tile):
    h = _rms(x_ref[...], g_ref[...]).astype(BF16)
    k_ref[...] = jnp.dot(h, wk_ref[...], preferred_element_type=F32).astype(BF16)
    qt = (lax.dot_general(wqt_ref[...], h, _NT, preferred_element_type=F32) * LOG2E).astype(BF16)
    vt = lax.dot_general(wvt_ref[...], h, _NT, preferred_element_type=F32).astype(BF16)
    ones = jnp.ones((VT_ROWS - HEAD_LANES, tile), BF16)
    for i in range(TM_QKV // tile):
        cols = slice(i * tile, (i + 1) * tile)
        for grp in range(VT_GROUPS):
            rows = slice(grp * HEAD_LANES, (grp + 1) * HEAD_LANES)
            qt_ref[i, grp] = qt[rows, cols]
            vt_ref[i, grp, :HEAD_LANES, :] = vt[rows, cols]
            vt_ref[i, grp, HEAD_LANES:, :] = ones


def _proj(x2, g, wqt, wk, wvt, tile):
    n = x2.shape[0]
    per_step = TM_QKV // tile
    fixed = lambda i: (0, 0)
    return pl.pallas_call(
        functools.partial(_proj_kernel, tile=tile),
        grid=(n // TM_QKV,),
        in_specs=[
            pl.BlockSpec((TM_QKV, D_MODEL), lambda i: (i, 0)),
            pl.BlockSpec((1, D_MODEL), fixed),
            pl.BlockSpec((D_MODEL, D_MODEL), fixed),
            pl.BlockSpec((D_MODEL, D_MODEL), fixed),
            pl.BlockSpec((D_MODEL, D_MODEL), fixed),
        ],
        out_specs=[
            pl.BlockSpec((per_step, VT_GROUPS, HEAD_LANES, tile), lambda i: (i, 0, 0, 0)),
            pl.BlockSpec((TM_QKV, D_MODEL), lambda i: (i, 0)),
            pl.BlockSpec((per_step, VT_GROUPS, VT_ROWS, tile), lambda i: (i, 0, 0, 0)),
        ],
        out_shape=[
            jax.ShapeDtypeStruct((n // tile, VT_GROUPS, HEAD_LANES, tile), BF16),
            jax.ShapeDtypeStruct((n, D_MODEL), BF16),
            jax.ShapeDtypeStruct((n // tile, VT_GROUPS, VT_ROWS, tile), BF16),
        ],
        compiler_params=pltpu.CompilerParams(
            dimension_semantics=("arbitrary",), vmem_limit_bytes=VMEM_LIMIT),
        name="qkv_proj",
    )(x2, g.reshape(1, D_MODEL), wqt, wk, wvt)


def _post_ffn_kernel(a_ref, x_ref, wo_ref, gpost_ref, gfpre_ref, gfpost_ref,
                     wg_ref, wu_ref, wd_ref, o_ref):
    m = jnp.dot(a_ref[...], wo_ref[...], preferred_element_type=F32)
    x1 = x_ref[...] + _rms(m, gpost_ref[...])
    h = _rms(x1, gfpre_ref[...]).astype(BF16)
    gate = jnp.dot(h, wg_ref[...], preferred_element_type=F32)
    up = jnp.dot(h, wu_ref[...], preferred_element_type=F32)
    act = (gate * (1.0 / (1.0 + jnp.exp(-gate))) * up).astype(BF16)
    f = jnp.dot(act, wd_ref[...], preferred_element_type=F32)
    o_ref[...] = x1 + _rms(f, gfpost_ref[...])


def _post_ffn(att2, x2, wo, gpost, gfpre, gfpost, wg, wu, wd):
    n = x2.shape[0]
    row = lambda i: (i, 0)
    fixed = lambda i: (0, 0)
    once = pl.Buffered(1)
    vec = pl.BlockSpec((1, D_MODEL), fixed)
    return pl.pallas_call(
        _post_ffn_kernel,
        grid=(n // TM_FFN,),
        in_specs=[
            pl.BlockSpec((TM_FFN, D_MODEL), row),
            pl.BlockSpec((TM_FFN, D_MODEL), row),
            pl.BlockSpec((D_MODEL, D_MODEL), fixed, pipeline_mode=once),
            vec, vec, vec,
            pl.BlockSpec((D_MODEL, D_FF), fixed, pipeline_mode=once),
            pl.BlockSpec((D_MODEL, D_FF), fixed, pipeline_mode=once),
            pl.BlockSpec((D_FF, D_MODEL), fixed, pipeline_mode=once),
        ],
        out_specs=pl.BlockSpec((TM_FFN, D_MODEL), row),
        out_shape=jax.ShapeDtypeStruct((n, D_MODEL), F32),
        compiler_params=pltpu.CompilerParams(
            dimension_semantics=("arbitrary",), vmem_limit_bytes=VMEM_LIMIT),
        name="post_ffn",
    )(att2, x2, wo, gpost.reshape(1, D_MODEL), gfpre.reshape(1, D_MODEL),
      gfpost.reshape(1, D_MODEL), wg, wu, wd)


def _na_bias_tiles(rpb):
    rpb = rpb.astype(F32) * LOG2E
    a = np.arange(NA_RQ)
    w = np.arange(NA_WK)
    qc = np.arange(GRID_W)
    kc = np.arange(GRID_W)
    q_start = np.clip(qc - NA_KW // 2, 0, GRID_W - NA_KW)
    col_ok = (kc[:, None] >= q_start[None, :]) & (kc[:, None] < q_start[None, :] + NA_KW)
    dc = np.clip(kc[:, None] - qc[None, :] + (NA_KW - 1), 0, 2 * NA_KW - 2)
    by_col = jnp.where(col_ok, rpb[:, :, dc], NEG_INF)
    masked_row = 2 * NA_KH - 1
    by_col = jnp.concatenate([by_col, jnp.full_like(by_col[:, :1], NEG_INF)], axis=1)
    offs = (0, NA_KH // 2, NA_WK - NA_RQ)
    los = (np.zeros_like(a), a, np.full_like(a, NA_WK - NA_KH))
    dr = np.stack([np.where((w[:, None] >= lo[None, :]) & (w[:, None] < lo[None, :] + NA_KH),
                            w[:, None] - (a[None, :] + off) + (NA_KH - 1), masked_row)
                   for off, lo in zip(offs, los)])
    t = by_col[:, dr]
    t = t.reshape(NA_HEADS // 2, 2, 3, NA_WK, NA_RQ, GRID_W, GRID_W)
    return jnp.transpose(t, (2, 0, 3, 5, 1, 4, 6)).reshape(3, NA_HEADS // 2, NA_TK, 2 * NA_TQ)


def _na_kernel(qt_ref, k_ref, vt_ref, bias_ref, o_ref, *, nblk, group):
    g = pl.program_id(2)
    first_rows = lax.broadcasted_iota(jnp.int32, (HEAD_LANES, NA_TQ), 0) < NA_HEAD_DIM
    ktiles = NA_TK // NA_TQ

    def window(u):
        blk = g * group + u
        return blk, jnp.clip(blk - 1, 0, nblk - ktiles)

    def scores(u, hh):
        blk, kt0 = window(u)
        cls = jnp.where(blk == 0, 0, jnp.where(blk == nblk - 1, 2, 1))
        qt = qt_ref[u, 0]
        zero = jnp.zeros_like(qt)
        qh = jnp.where(first_rows, qt, zero) if hh == 0 else jnp.where(first_rows, zero, qt)
        k = k_ref[0, pl.ds(pl.multiple_of(kt0 * NA_TQ, NA_TQ), NA_TK), :]
        return (jnp.dot(k, qh, preferred_element_type=F32)
                + bias_ref[cls, 0, :, hh * NA_TQ:(hh + 1) * NA_TQ])

    def attend(u, s):
        _, kt0 = window(u)
        vt = jnp.concatenate([vt_ref[kt0 + i, 0] for i in range(ktiles)], axis=1)
        p = jnp.exp2(s - jnp.max(s, axis=0, keepdims=True)).astype(BF16)
        return jnp.dot(vt, p, preferred_element_type=F32)

    def write(u, o):
        num = jnp.where(first_rows, o[0][:HEAD_LANES], o[1][:HEAD_LANES])
        den = jnp.where(first_rows, o[0][HEAD_LANES:HEAD_LANES + 1], o[1][HEAD_LANES:HEAD_LANES + 1])
        o_ref[0, u * NA_TQ:(u + 1) * NA_TQ, :] = jnp.transpose(num / den).astype(BF16)

    pending = [scores(0, hh) for hh in range(2)]
    for u in range(1, group):
        outs = []
        for hh in range(2):
            ahead = scores(u, hh)
            outs.append(attend(u - 1, pending[hh]))
            pending[hh] = ahead
        write(u - 1, outs)
    write(group - 1, [attend(group - 1, pending[hh]) for hh in range(2)])


def _na_attention(qt, k3, vt, bias):
    b, s, _ = k3.shape
    nblk = s // NA_TQ
    group = math.gcd(NA_G, nblk)
    steps = nblk // group
    assert nblk >= NA_TK // NA_TQ and s % NA_TQ == 0
    npair = NA_HEADS // 2
    return pl.pallas_call(
        functools.partial(_na_kernel, nblk=nblk, group=group),
        grid=(npair, b, steps),
        in_specs=[
            pl.BlockSpec((group, 1, HEAD_LANES, NA_TQ), lambda hp, bb, i: (bb * steps + i, hp, 0, 0)),
            pl.BlockSpec((1, s, HEAD_LANES), lambda hp, bb, i: (bb, 0, hp)),
            pl.BlockSpec((nblk, 1, VT_ROWS, NA_TQ), lambda hp, bb, i: (bb, hp, 0, 0)),
            pl.BlockSpec((3, 1, NA_TK, 2 * NA_TQ), lambda hp, bb, i: (0, hp, 0, 0)),
        ],
        out_specs=pl.BlockSpec((1, group * NA_TQ, HEAD_LANES), lambda hp, bb, i: (bb, i, hp)),
        out_shape=jax.ShapeDtypeStruct((b, s, D_MODEL), BF16),
        compiler_params=pltpu.CompilerParams(
            dimension_semantics=("arbitrary", "arbitrary", "arbitrary"),
            vmem_limit_bytes=VMEM_LIMIT),
        name="na_attention",
    )(qt, k3, vt, bias)


def _t5_bucket(rel):
    nb = T5_BUCKETS // 2
    max_exact = nb // 2
    ret = jnp.where(rel > 0, nb, 0)
    n = jnp.abs(rel)
    nf = jnp.maximum(n, max_exact).astype(F32)
    large = max_exact + (jnp.log(nf / max_exact) / math.log(T5_MAX_DIST / max_exact)
                         * (nb - max_exact)).astype(jnp.int32)
    large = jnp.minimum(large, nb - 1)
    return ret + jnp.where(n < max_exact, n, large)


def _diff_bias_tiles(t5_bias):
    table = t5_bias.astype(F32)
    t = DIFF_T
    nd = 2 * DIFF_NEAR + 1
    i = jnp.arange(2 * t)
    d = jnp.arange(-DIFF_NEAR, DIFF_NEAR + 1)
    rel = d[:, None] * t + (t - 1) - i[None, :]
    u = jnp.transpose(table[_t5_bucket(rel)], (2, 0, 1))
    skew = jnp.broadcast_to(u[:, :, None, :], (DIFF_HEADS, nd, t, 2 * t)).reshape(DIFF_HEADS, nd, 2 * t * t)
    skew = skew[:, :, :t * (2 * t - 1)].reshape(DIFF_HEADS, nd, t, 2 * t - 1)
    return skew[:, :, :, t - 1:]


def _diff_kernel(far_ref, lam_ref, qt_ref, k_ref, vt_ref, nb_ref, g_ref, o_ref, s_even, s_odd, acc_scr,
                 *, nk, post_scale):
    t = DIFF_T
    h = pl.program_id(0)
    qi = pl.program_id(2)
    qt = qt_ref[0, 0]
    first = lax.broadcasted_iota(jnp.int32, (HEAD_LANES, t), 0) < DIFF_HEAD_DIM
    zero = jnp.zeros_like(qt)
    qs = (jnp.where(first, qt, zero), jnp.where(first, zero, qt))
    c_left = far_ref[h, 0]
    c_right = far_ref[h, 1]

    s_slots = (s_even, s_odd)

    parts = [(c, slice(h * t // DIFF_QSPLIT, (h + 1) * t // DIFF_QSPLIT))
             for c in range(2) for h in range(DIFF_QSPLIT)]

    def scores(j, slot, part, bias_tile, bias_const):
        c, cols = part
        k = k_ref[0, pl.ds(pl.multiple_of(j * t, t), t), :]
        s = jnp.dot(k, qs[c][:, cols], preferred_element_type=F32)
        if bias_tile is not None:
            s = s + bias_tile[:, cols]
        s_slots[slot][c, :, cols] = s
        cm = jnp.max(s, axis=0, keepdims=True)
        return cm if bias_const is None else cm + bias_const

    def consume(j, slot, part, cm, bias_const, m_old):
        c, cols = part
        vt = vt_ref[j, 0]
        m_new = jnp.maximum(m_old, cm)
        p = jnp.exp2(s_slots[slot][c, :, cols] - (m_new - bias_const)).astype(BF16)
        alpha = jnp.exp2(m_old - m_new)
        acc_scr[c, :, cols] = alpha * acc_scr[c, :, cols] + jnp.dot(vt, p, preferred_element_type=F32)
        return m_new

    def step(j, slot, state, bias_tile, bias_const):
        cm_prev, c_prev, m_prev = state
        cms, ms = [], []
        for i, part in enumerate(parts):
            cms.append(scores(j, slot, part, bias_tile, bias_const))
            ms.append(consume(j - 1, 1 - slot, part, cm_prev[i], c_prev, m_prev[i]))
        c_new = jnp.float32(0.0) if bias_const is None else bias_const
        return (tuple(cms), c_new, tuple(ms))

    def tile_loop(start, stop, state, near, bias_const):
        def run(first, trips, width, st0):
            def body(i, st):
                j = first + width * i
                for d in range(width):
                    tile = nb_ref[0, j + d - qi + DIFF_NEAR] if near else None
                    st = step(j + d, (1 + d) % 2, st, tile, bias_const)
                return st
            return lax.fori_loop(0, trips, body, st0)
        for width in ((2 * DIFF_NEAR, 2) if near else (DIFF_UNROLL, 2 * DIFF_NEAR, 2)):
            trips = (stop - start) // width
            state = run(start, trips, width, state)
            start = start + width * trips
        return state

    m0 = jnp.full((1, t // DIFF_QSPLIT), NEG_INF, F32)
    acc_scr[...] = jnp.zeros_like(acc_scr)
    last = nk - 1
    near_lo = qi - 1 - (qi % 2)
    near_hi = near_lo + 2 * DIFF_NEAR
    bias_first = jnp.where(near_lo <= 0, nb_ref[0, jnp.clip(DIFF_NEAR - qi, 0, 2 * DIFF_NEAR)], c_left)
    state = (tuple(scores(0, 0, part, bias_first, None) for part in parts), jnp.float32(0.0),
             (m0,) * len(parts))
    near_start = jnp.maximum(near_lo, 1)
    near_stop = jnp.minimum(near_hi, last)
    state = tile_loop(1, near_start, state, False, c_left)
    state = tile_loop(near_start, near_stop, state, True, None)
    state = tile_loop(near_stop, last, state, False, c_right)
    bias_last = jnp.where(near_hi > last,
                          nb_ref[0, jnp.clip(last - qi + DIFF_NEAR, 0, 2 * DIFF_NEAR)], c_right)
    state = step(last, 1, state, bias_last, None)
    cm_prev, c_prev, m_prev = state
    for i, part in enumerate(parts):
        consume(last, 1, part, cm_prev[i], c_prev, m_prev[i])
    num = [acc_scr[c, :HEAD_LANES, :] for c in range(2)]
    den = [acc_scr[c, HEAD_LANES:HEAD_LANES + 1, :] for c in range(2)]
    out = num[0] / den[0] - lam_ref[0] * (num[1] / den[1])
    out = out * lax.rsqrt(jnp.mean(out * out, axis=0, keepdims=True) + RMS_EPS) * g_ref[...]
    o_ref[0] = jnp.transpose(out * post_scale).astype(BF16)


def _diff_attention(qt, k3, vt, near_bias, far_bias, lam, subln_g, lambda_init):
    b, s, _ = k3.shape
    t = DIFF_T
    nk = s // t
    assert nk >= 2 and nk % 2 == 0 and t > T5_MAX_DIST
    nh = DIFF_HEADS
    grid_spec = pltpu.PrefetchScalarGridSpec(
        num_scalar_prefetch=2,
        grid=(nh, b, nk),
        in_specs=[
            pl.BlockSpec((1, 1, HEAD_LANES, t), lambda h, bb, i, *_: (bb * nk + i, h, 0, 0)),
            pl.BlockSpec((1, s, HEAD_LANES), lambda h, bb, i, *_: (bb, 0, h)),
            pl.BlockSpec((nk, 1, VT_ROWS, t), lambda h, bb, i, *_: (bb, h, 0, 0)),
            pl.BlockSpec((1, 2 * DIFF_NEAR + 1, t, t), lambda h, bb, i, *_: (h, 0, 0, 0)),
            pl.BlockSpec((HEAD_LANES, 1), lambda h, bb, i, *_: (0, 0)),
        ],
        out_specs=pl.BlockSpec((1, t, HEAD_LANES), lambda h, bb, i, *_: (bb, i, h)),
        scratch_shapes=[
            pltpu.VMEM((2, t, t), F32),
            pltpu.VMEM((2, t, t), F32),
            pltpu.VMEM((2, VT_ROWS, t), F32),
        ],
    )
    return pl.pallas_call(
        functools.partial(_diff_kernel, nk=nk, post_scale=1.0 - lambda_init),
        grid_spec=grid_spec,
        out_shape=jax.ShapeDtypeStruct((b, s, D_MODEL), BF16),
        compiler_params=pltpu.CompilerParams(
            dimension_semantics=("arbitrary", "arbitrary", "arbitrary"),
            vmem_limit_bytes=VMEM_LIMIT),
        name="diff_attention",
    )(far_bias, lam, qt, k3, vt, near_bias, subln_g.reshape(HEAD_LANES, 1))


def _split_qkv(w, head_dim):
    scale = head_dim ** -0.5
    assert math.frexp(scale)[0] == 0.5
    d = D_MODEL
    return (jnp.transpose(w[:, :d] * scale).astype(BF16), w[:, d:2 * d].astype(BF16),
            jnp.transpose(w[:, 2 * d:]).astype(BF16))


def kernel(x_prompt, x_sample, mix_pre_g, mix_post_g, na_w_qkv, na_w_o, na_rpb, diff_w_qkv, diff_w_o,
           diff_lambda_q1, diff_lambda_k1, diff_lambda_q2, diff_lambda_k2, diff_subln_g, t5_bias,
           ffn_pre_g, ffn_post_g, ffn_w_gate, ffn_w_up, ffn_w_down):
    na_bias = [_na_bias_tiles(na_rpb[li]) for li in range(na_rpb.shape[0])]
    table = t5_bias.astype(F32) * LOG2E
    diff_near = _diff_bias_tiles(table)
    far = jnp.stack([table[T5_BUCKETS // 2 - 1], table[T5_BUCKETS - 1]], axis=1)
    wg = ffn_w_gate.astype(BF16)
    wu = ffn_w_up.astype(BF16)
    wd = ffn_w_down.astype(BF16)

    def trunk(x):
        b, s, d = x.shape
        x2 = x.reshape(b * s, d)
        for i in range(DEPTH):
            li = i // 2
            if i % 2 == 0:
                qt, k, vt = _proj(x2, mix_pre_g[i], *_split_qkv(na_w_qkv[li], NA_HEAD_DIM), NA_TQ)
                att = _na_attention(qt, k.reshape(b, s, d), vt, na_bias[li])
                w_o = na_w_o[li]
            else:
                lambda_init = 0.8 - 0.6 * math.exp(-0.3 * i)
                lam = (jnp.exp(jnp.sum(diff_lambda_q1[li].astype(F32) * diff_lambda_k1[li].astype(F32)))
                       - jnp.exp(jnp.sum(diff_lambda_q2[li].astype(F32) * diff_lambda_k2[li].astype(F32)))
                       + lambda_init).reshape(1)
                qt, k, vt = _proj(x2, mix_pre_g[i], *_split_qkv(diff_w_qkv[li], DIFF_HEAD_DIM), DIFF_T)
                att = _diff_attention(qt, k.reshape(b, s, d), vt, diff_near, far, lam,
                                      diff_subln_g[li], lambda_init)
                w_o = diff_w_o[li]
            x2 = _post_ffn(att.reshape(b * s, d), x2, w_o.astype(BF16), mix_post_g[i],
                           ffn_pre_g[i], ffn_post_g[i], wg[i], wu[i], wd[i])
        return x2.reshape(b, s, d)

    return (trunk(x_prompt), trunk(x_sample))
```

```python
import functools
import math

import jax
import jax.numpy as jnp
import numpy as np
from jax import lax
from jax.experimental import pallas as pl
from jax.experimental.pallas import tpu as pltpu

F32 = jnp.float32
BF16 = jnp.bfloat16

D_MODEL = 1024
DEPTH = 2
RMS_EPS = 1e-6
NEG_INF = -1e30

GRID_W = 64
NA_HEADS = 16
NA_HEAD_DIM = 64
NA_KH = 8
NA_KW = 16
NA_RQ = 4
NA_WK = 3 * NA_RQ
NA_TQ = NA_RQ * GRID_W
NA_TK = NA_WK * GRID_W
NA_G = 16

DIFF_HEADS = 8
DIFF_HEAD_DIM = 64
DIFF_T = 512
T5_BUCKETS = 32
T5_MAX_DIST = 128
DIFF_NEAR = 2
DIFF_UNROLL = 8
DIFF_QSPLIT = 2
DIFF_QTILES = 2

HEAD_LANES = 128
VT_GROUPS = D_MODEL // HEAD_LANES
VT_ROWS = HEAD_LANES + 16
D_FF = 2816

TM_QKV = 512
TM_FFN = 512
VMEM_LIMIT = 56 * 1024 * 1024

_NT = (((1,), (1,)), ((), ()))
LOG2E = math.log2(math.e)


def _rms(x, g):
    return x * lax.rsqrt(jnp.mean(x * x, axis=-1, keepdims=True) + RMS_EPS) * g


def _proj_kernel(x_ref, g_ref, wqt_ref, wk_ref, wvt_ref, qt_ref, k_ref, vt_ref, *, tile):
    h = _rms(x_ref[...], g_ref[...]).astype(BF16)
    k_ref[...] = jnp.dot(h, wk_ref[...], preferred_element_type=F32).astype(BF16)
    qt = (lax.dot_general(wqt_ref[...], h, _NT, preferred_element_type=F32) * LOG2E).astype(BF16)
    vt = lax.dot_general(wvt_ref[...], h, _NT, preferred_element_type=F32).astype(BF16)
    ones = jnp.ones((VT_ROWS - HEAD_LANES, tile), BF16)
    for i in range(TM_QKV // tile):
        cols = slice(i * tile, (i + 1) * tile)
        for grp in range(VT_GROUPS):
            rows = slice(grp * HEAD_LANES, (grp + 1) * HEAD_LANES)
            qt_ref[i, grp] = qt[rows, cols]
            vt_ref[i, grp, :HEAD_LANES, :] = vt[rows, cols]
            vt_ref[i, grp, HEAD_LANES:, :] = ones


def _proj(x2, g, wqt, wk, wvt, tile):
    n = x2.shape[0]
    per_step = TM_QKV // tile
    fixed = lambda i: (0, 0)
    return pl.pallas_call(
        functools.partial(_proj_kernel, tile=tile),
        grid=(n // TM_QKV,),
        in_specs=[
            pl.BlockSpec((TM_QKV, D_MODEL), lambda i: (i, 0)),
            pl.BlockSpec((1, D_MODEL), fixed),
            pl.BlockSpec((D_MODEL, D_MODEL), fixed),
            pl.BlockSpec((D_MODEL, D_MODEL), fixed),
            pl.BlockSpec((D_MODEL, D_MODEL), fixed),
        ],
        out_specs=[
            pl.BlockSpec((per_step, VT_GROUPS, HEAD_LANES, tile), lambda i: (i, 0, 0, 0)),
            pl.BlockSpec((TM_QKV, D_MODEL), lambda i: (i, 0)),
            pl.BlockSpec((per_step, VT_GROUPS, VT_ROWS, tile), lambda i: (i, 0, 0, 0)),
        ],
        out_shape=[
            jax.ShapeDtypeStruct((n // tile, VT_GROUPS, HEAD_LANES, tile), BF16),
            jax.ShapeDtypeStruct((n, D_MODEL), BF16),
            jax.ShapeDtypeStruct((n // tile, VT_GROUPS, VT_ROWS, tile), BF16),
        ],
        compiler_params=pltpu.CompilerParams(
            dimension_semantics=("arbitrary",), vmem_limit_bytes=VMEM_LIMIT),
        name="qkv_proj",
    )(x2, g.reshape(1, D_MODEL), wqt, wk, wvt)


def _post_ffn_kernel(a_ref, x_ref, wo_ref, gpost_ref, gfpre_ref, gfpost_ref,
                     wg_ref, wu_ref, wd_ref, o_ref):
    m = jnp.dot(a_ref[...], wo_ref[...], preferred_element_type=F32)
    x1 = x_ref[...] + _rms(m, gpost_ref[...])
    h = _rms(x1, gfpre_ref[...]).astype(BF16)
    gate = jnp.dot(h, wg_ref[...], preferred_element_type=F32)
    up = jnp.dot(h, wu_ref[...], preferred_element_type=F32)
    act = (gate * (1.0 / (1.0 + jnp.exp(-gate))) * up).astype(BF16)
    f = jnp.dot(act, wd_ref[...], preferred_element_type=F32)
    o_ref[...] = x1 + _rms(f, gfpost_ref[...])


def _post_ffn(att2, x2, wo, gpost, gfpre, gfpost, wg, wu, wd):
    n = x2.shape[0]
    row = lambda i: (i, 0)
    fixed = lambda i: (0, 0)
    once = pl.Buffered(1)
    vec = pl.BlockSpec((1, D_MODEL), fixed)
    return pl.pallas_call(
        _post_ffn_kernel,
        grid=(n // TM_FFN,),
        in_specs=[
            pl.BlockSpec((TM_FFN, D_MODEL), row),
            pl.BlockSpec((TM_FFN, D_MODEL), row),
            pl.BlockSpec((D_MODEL, D_MODEL), fixed, pipeline_mode=once),
            vec, vec, vec,
            pl.BlockSpec((D_MODEL, D_FF), fixed, pipeline_mode=once),
            pl.BlockSpec((D_MODEL, D_FF), fixed, pipeline_mode=once),
            pl.BlockSpec((D_FF, D_MODEL), fixed, pipeline_mode=once),
        ],
        out_specs=pl.BlockSpec((TM_FFN, D_MODEL), row),
        out_shape=jax.ShapeDtypeStruct((n, D_MODEL), F32),
        compiler_params=pltpu.CompilerParams(
            dimension_semantics=("arbitrary",), vmem_limit_bytes=VMEM_LIMIT),
        name="post_ffn",
    )(att2, x2, wo, gpost.reshape(1, D_MODEL), gfpre.reshape(1, D_MODEL),
      gfpost.reshape(1, D_MODEL), wg, wu, wd)


def _na_bias_tiles(rpb):
    rpb = rpb.astype(F32) * LOG2E
    a = np.arange(NA_RQ)
    w = np.arange(NA_WK)
    qc = np.arange(GRID_W)
    kc = np.arange(GRID_W)
    q_start = np.clip(qc - NA_KW // 2, 0, GRID_W - NA_KW)
    col_ok = (kc[:, None] >= q_start[None, :]) & (kc[:, None] < q_start[None, :] + NA_KW)
    dc = np.clip(kc[:, None] - qc[None, :] + (NA_KW - 1), 0, 2 * NA_KW - 2)
    by_col = jnp.where(col_ok, rpb[:, :, dc], NEG_INF)
    masked_row = 2 * NA_KH - 1
    by_col = jnp.concatenate([by_col, jnp.full_like(by_col[:, :1], NEG_INF)], axis=1)
    offs = (0, NA_KH // 2, NA_WK - NA_RQ)
    los = (np.zeros_like(a), a, np.full_like(a, NA_WK - NA_KH))
    dr = np.stack([np.where((w[:, None] >= lo[None, :]) & (w[:, None] < lo[None, :] + NA_KH),
                            w[:, None] - (a[None, :] + off) + (NA_KH - 1), masked_row)
                   for off, lo in zip(offs, los)])
    t = by_col[:, dr]
    t = t.reshape(NA_HEADS // 2, 2, 3, NA_WK, NA_RQ, GRID_W, GRID_W)
    return jnp.transpose(t, (2, 0, 3, 5, 1, 4, 6)).reshape(3, NA_HEADS // 2, NA_TK, 2 * NA_TQ)


def _na_kernel(qt_ref, k_ref, vt_ref, bias_ref, o_ref, *, nblk, group):
    g = pl.program_id(2)
    first_rows = lax.broadcasted_iota(jnp.int32, (HEAD_LANES, NA_TQ), 0) < NA_HEAD_DIM
    ktiles = NA_TK // NA_TQ

    def window(u):
        blk = g * group + u
        return blk, jnp.clip(blk - 1, 0, nblk - ktiles)

    def scores(u, hh):
        blk, kt0 = window(u)
        cls = jnp.where(blk == 0, 0, jnp.where(blk == nblk - 1, 2, 1))
        qt = qt_ref[u, 0]
        zero = jnp.zeros_like(qt)
        qh = jnp.where(first_rows, qt, zero) if hh == 0 else jnp.where(first_rows, zero, qt)
        k = k_ref[0, pl.ds(pl.multiple_of(kt0 * NA_TQ, NA_TQ), NA_TK), :]
        return (jnp.dot(k, qh, preferred_element_type=F32)
                + bias_ref[cls, 0, :, hh * NA_TQ:(hh + 1) * NA_TQ])

    def attend(u, s):
        _, kt0 = window(u)
        vt = jnp.concatenate([vt_ref[kt0 + i, 0] for i in range(ktiles)], axis=1)
        p = jnp.exp2(s - jnp.max(s, axis=0, keepdims=True)).astype(BF16)
        return jnp.dot(vt, p, preferred_element_type=F32)

    def write(u, o):
        num = jnp.where(first_rows, o[0][:HEAD_LANES], o[1][:HEAD_LANES])
        den = jnp.where(first_rows, o[0][HEAD_LANES:HEAD_LANES + 1], o[1][HEAD_LANES:HEAD_LANES + 1])
        o_ref[0, u * NA_TQ:(u + 1) * NA_TQ, :] = jnp.transpose(num / den).astype(BF16)

    pending = [scores(0, hh) for hh in range(2)]
    for u in range(1, group):
        outs = []
        for hh in range(2):
            ahead = scores(u, hh)
            outs.append(attend(u - 1, pending[hh]))
            pending[hh] = ahead
        write(u - 1, outs)
    write(group - 1, [attend(group - 1, pending[hh]) for hh in range(2)])


def _na_attention(qt, k3, vt, bias):
    b, s, _ = k3.shape
    nblk = s // NA_TQ
    group = math.gcd(NA_G, nblk)
    steps = nblk // group
    assert nblk >= NA_TK // NA_TQ and s % NA_TQ == 0
    npair = NA_HEADS // 2
    return pl.pallas_call(
        functools.partial(_na_kernel, nblk=nblk, group=group),
        grid=(npair, b, steps),
        in_specs=[
            pl.BlockSpec((group, 1, HEAD_LANES, NA_TQ), lambda hp, bb, i: (bb * steps + i, hp, 0, 0)),
            pl.BlockSpec((1, s, HEAD_LANES), lambda hp, bb, i: (bb, 0, hp)),
            pl.BlockSpec((nblk, 1, VT_ROWS, NA_TQ), lambda hp, bb, i: (bb, hp, 0, 0)),
            pl.BlockSpec((3, 1, NA_TK, 2 * NA_TQ), lambda hp, bb, i: (0, hp, 0, 0)),
        ],
        out_specs=pl.BlockSpec((1, group * NA_TQ, HEAD_LANES), lambda hp, bb, i: (bb, i, hp)),
        out_shape=jax.ShapeDtypeStruct((b, s, D_MODEL), BF16),
        compiler_params=pltpu.CompilerParams(
            dimension_semantics=("arbitrary", "arbitrary", "arbitrary"),
            vmem_limit_bytes=VMEM_LIMIT),
        name="na_attention",
    )(qt, k3, vt, bias)


def _t5_bucket(rel):
    nb = T5_BUCKETS // 2
    max_exact = nb // 2
    ret = jnp.where(rel > 0, nb, 0)
    n = jnp.abs(rel)
    nf = jnp.maximum(n, max_exact).astype(F32)
    large = max_exact + (jnp.log(nf / max_exact) / math.log(T5_MAX_DIST / max_exact)
                         * (nb - max_exact)).astype(jnp.int32)
    large = jnp.minimum(large, nb - 1)
    return ret + jnp.where(n < max_exact, n, large)


def _diff_bias_tiles(t5_bias):
    table = t5_bias.astype(F32)
    t, blk = DIFF_T, T5_MAX_DIST
    nh = DIFF_HEADS
    i = jnp.arange(2 * blk)
    cs = jnp.array([-blk, 0, blk])
    rel = cs[:, None] + (blk - 1) - i[None, :]
    u = jnp.transpose(table[_t5_bucket(rel)], (2, 0, 1))
    skew = jnp.broadcast_to(u[:, :, None, :], (nh, 3, blk, 2 * blk)).reshape(nh, 3, 2 * blk * blk)
    skew = skew[:, :, :blk * (2 * blk - 1)].reshape(nh, 3, blk, 2 * blk - 1)[:, :, :, blk - 1:]
    left = jnp.broadcast_to(table[T5_BUCKETS // 2 - 1][:, None, None, None], (nh, 1, blk, blk))
    right = jnp.broadcast_to(table[T5_BUCKETS - 1][:, None, None, None], (nh, 1, blk, blk))
    blocks = jnp.concatenate([left, skew, right], axis=1)
    nb = t // blk
    d = np.arange(-DIFF_NEAR, DIFF_NEAR + 1)[:, None, None]
    c = d * t + blk * (np.arange(nb)[None, :, None] - np.arange(nb)[None, None, :])
    which = np.clip(c // blk + 2, 0, 4)
    tiles = blocks[:, which]
    return jnp.transpose(tiles, (0, 1, 2, 4, 3, 5)).reshape(nh, 2 * DIFF_NEAR + 1, t, t)


def _diff_query_tile(sub, far_ref, lam_ref, qt_ref, k_ref, vt_ref, nb_ref, g_ref, o_ref, s_even, s_odd,
                     acc_scr, *, nk, post_scale):
    t = DIFF_T
    h = pl.program_id(0)
    qi = pl.program_id(2) * DIFF_QTILES + sub
    qt = qt_ref[sub, 0]
    first = lax.broadcasted_iota(jnp.int32, (HEAD_LANES, t), 0) < DIFF_HEAD_DIM
    zero = jnp.zeros_like(qt)
    qs = (jnp.where(first, qt, zero), jnp.where(first, zero, qt))
    c_left = far_ref[h, 0]
    c_right = far_ref[h, 1]

    s_slots = (s_even, s_odd)

    parts = [(c, slice(h * t // DIFF_QSPLIT, (h + 1) * t // DIFF_QSPLIT))
             for c in range(2) for h in range(DIFF_QSPLIT)]

    def scores(j, slot, part, bias_tile, bias_const):
        c, cols = part
        k = k_ref[0, pl.ds(pl.multiple_of(j * t, t), t), :]
        s = jnp.dot(k, qs[c][:, cols], preferred_element_type=F32)
        if bias_tile is not None:
            s = s + bias_tile[:, cols]
        s_slots[slot][c, :, cols] = s
        cm = jnp.max(s, axis=0, keepdims=True)
        return cm if bias_const is None else cm + bias_const

    def consume(j, slot, part, cm, bias_const, m_old):
        c, cols = part
        vt = vt_ref[j, 0]
        m_new = jnp.maximum(m_old, cm)
        p = jnp.exp2(s_slots[slot][c, :, cols] - (m_new - bias_const)).astype(BF16)
        alpha = jnp.exp2(m_old - m_new)
        acc_scr[sub, c, :, cols] = alpha * acc_scr[sub, c, :, cols] + jnp.dot(vt, p, preferred_element_type=F32)
        return m_new

    def step(j, slot, state, bias_tile, bias_const):
        cm_prev, c_prev, m_prev = state
        cms, ms = [], []
        for i, part in enumerate(parts):
            cms.append(scores(j, slot, part, bias_tile, bias_const))
            ms.append(consume(j - 1, 1 - slot, part, cm_prev[i], c_prev, m_prev[i]))
        c_new = jnp.float32(0.0) if bias_const is None else bias_const
        return (tuple(cms), c_new, tuple(ms))

    def tile_loop(start, stop, state, near, bias_const):
        def run(first, trips, width, st0):
            def body(i, st):
                j = first + width * i
                for d in range(width):
                    tile = nb_ref[0, j + d - qi + DIFF_NEAR] if near else None
                    st = step(j + d, (1 + d) % 2, st, tile, bias_const)
                return st
            return lax.fori_loop(0, trips, body, st0)
        for width in ((2 * DIFF_NEAR, 2) if near else (DIFF_UNROLL, 2 * DIFF_NEAR, 2)):
            trips = (stop - start) // width
            state = run(start, trips, width, state)
            start = start + width * trips
        return state

    m0 = jnp.full((1, t // DIFF_QSPLIT), NEG_INF, F32)
    acc_scr[sub] = jnp.zeros_like(acc_scr[sub])
    last = nk - 1
    near_lo = qi - 1 - (qi % 2)
    near_hi = near_lo + 2 * DIFF_NEAR
    bias_first = jnp.where(near_lo <= 0, nb_ref[0, jnp.clip(DIFF_NEAR - qi, 0, 2 * DIFF_NEAR)], c_left)
    state = (tuple(scores(0, 0, part, bias_first, None) for part in parts), jnp.float32(0.0),
             (m0,) * len(parts))
    near_start = jnp.maximum(near_lo, 1)
    near_stop = jnp.minimum(near_hi, last)
    state = tile_loop(1, near_start, state, False, c_left)
    state = tile_loop(near_start, near_stop, state, True, None)
    state = tile_loop(near_stop, last, state, False, c_right)
    bias_last = jnp.where(near_hi > last,
                          nb_ref[0, jnp.clip(last - qi + DIFF_NEAR, 0, 2 * DIFF_NEAR)], c_right)
    state = step(last, 1, state, bias_last, None)
    cm_prev, c_prev, m_prev = state
    for i, part in enumerate(parts):
        consume(last, 1, part, cm_prev[i], c_prev, m_prev[i])
    num = [acc_scr[sub, c, :HEAD_LANES, :] for c in range(2)]
    den = [acc_scr[sub, c, HEAD_LANES:HEAD_LANES + 1, :] for c in range(2)]
    out = num[0] / den[0] - lam_ref[0] * (num[1] / den[1])
    out = out * lax.rsqrt(jnp.mean(out * out, axis=0, keepdims=True) + RMS_EPS) * g_ref[...]
    o_ref[0, sub * t:(sub + 1) * t, :] = jnp.transpose(out * post_scale).astype(BF16)


def _diff_kernel(*refs, nk, post_scale):
    for sub in range(DIFF_QTILES):
        _diff_query_tile(sub, *refs, nk=nk, post_scale=post_scale)


def _diff_attention(qt, k3, vt, near_bias, far_bias, lam, subln_g, lambda_init):
    b, s, _ = k3.shape
    t = DIFF_T
    nk = s // t
    assert nk >= 2 and nk % 2 == 0 and nk % DIFF_QTILES == 0 and t > T5_MAX_DIST
    steps = nk // DIFF_QTILES
    nh = DIFF_HEADS
    grid_spec = pltpu.PrefetchScalarGridSpec(
        num_scalar_prefetch=2,
        grid=(nh, b, steps),
        in_specs=[
            pl.BlockSpec((DIFF_QTILES, 1, HEAD_LANES, t), lambda h, bb, i, *_: (bb * steps + i, h, 0, 0)),
            pl.BlockSpec((1, s, HEAD_LANES), lambda h, bb, i, *_: (bb, 0, h)),
            pl.BlockSpec((nk, 1, VT_ROWS, t), lambda h, bb, i, *_: (bb, h, 0, 0)),
            pl.BlockSpec((1, 2 * DIFF_NEAR + 1, t, t), lambda h, bb, i, *_: (h, 0, 0, 0)),
            pl.BlockSpec((HEAD_LANES, 1), lambda h, bb, i, *_: (0, 0)),
        ],
        out_specs=pl.BlockSpec((1, DIFF_QTILES * t, HEAD_LANES), lambda h, bb, i, *_: (bb, i, h)),
        scratch_shapes=[
            pltpu.VMEM((2, t, t), F32),
            pltpu.VMEM((2, t, t), F32),
            pltpu.VMEM((DIFF_QTILES, 2, VT_ROWS, t), F32),
        ],
    )
    return pl.pallas_call(
        functools.partial(_diff_kernel, nk=nk, post_scale=1.0 - lambda_init),
        grid_spec=grid_spec,
        out_shape=jax.ShapeDtypeStruct((b, s, D_MODEL), BF16),
        compiler_params=pltpu.CompilerParams(
            dimension_semantics=("arbitrary", "arbitrary", "arbitrary"),
            vmem_limit_bytes=VMEM_LIMIT),
        name="diff_attention",
    )(far_bias, lam, qt, k3, vt, near_bias, subln_g.reshape(HEAD_LANES, 1))


def _split_qkv(w, head_dim):
    scale = head_dim ** -0.5
    assert math.frexp(scale)[0] == 0.5
    d = D_MODEL
    return (jnp.transpose(w[:, :d] * scale).astype(BF16), w[:, d:2 * d].astype(BF16),
            jnp.transpose(w[:, 2 * d:]).astype(BF16))


def kernel(x_prompt, x_sample, mix_pre_g, mix_post_g, na_w_qkv, na_w_o, na_rpb, diff_w_qkv, diff_w_o,
           diff_lambda_q1, diff_lambda_k1, diff_lambda_q2, diff_lambda_k2, diff_subln_g, t5_bias,
           ffn_pre_g, ffn_post_g, ffn_w_gate, ffn_w_up, ffn_w_down):
    na_bias = [_na_bias_tiles(na_rpb[li]) for li in range(na_rpb.shape[0])]
    table = t5_bias.astype(F32) * LOG2E
    diff_near = _diff_bias_tiles(table)
    far = jnp.stack([table[T5_BUCKETS // 2 - 1], table[T5_BUCKETS - 1]], axis=1)
    wg = ffn_w_gate.astype(BF16)
    wu = ffn_w_up.astype(BF16)
    wd = ffn_w_down.astype(BF16)

    def trunk(x):
        b, s, d = x.shape
        x2 = x.reshape(b * s, d)
        for i in range(DEPTH):
            li = i // 2
            if i % 2 == 0:
                qt, k, vt = _proj(x2, mix_pre_g[i], *_split_qkv(na_w_qkv[li], NA_HEAD_DIM), NA_TQ)
                att = _na_attention(qt, k.reshape(b, s, d), vt, na_bias[li])
                w_o = na_w_o[li]
            else:
                lambda_init = 0.8 - 0.6 * math.exp(-0.3 * i)
                lam = (jnp.exp(jnp.sum(diff_lambda_q1[li].astype(F32) * diff_lambda_k1[li].astype(F32)))
                       - jnp.exp(jnp.sum(diff_lambda_q2[li].astype(F32) * diff_lambda_k2[li].astype(F32)))
                       + lambda_init).reshape(1)
                qt, k, vt = _proj(x2, mix_pre_g[i], *_split_qkv(diff_w_qkv[li], DIFF_HEAD_DIM), DIFF_T)
                att = _diff_attention(qt, k.reshape(b, s, d), vt, diff_near, far, lam,
                                      diff_subln_g[li], lambda_init)
                w_o = diff_w_o[li]
            x2 = _post_ffn(att.reshape(b * s, d), x2, w_o.astype(BF16), mix_post_g[i],
                           ffn_pre_g[i], ffn_post_g[i], wg[i], wu[i], wd[i])
        return x2.reshape(b, s, d)

    return (trunk(x_prompt), trunk(x_sample))
```

```python
import functools
import math

import jax
import jax.numpy as jnp
import numpy as np
from jax import lax
from jax.experimental import pallas as pl
from jax.experimental.pallas import tpu as pltpu

F32 = jnp.float32
BF16 = jnp.bfloat16

D_MODEL = 1024
DEPTH = 2
RMS_EPS = 1e-6
NEG_INF = -1e30

GRID_W = 64
NA_HEADS = 16
NA_HEAD_DIM = 64
NA_KH = 8
NA_KW = 16
NA_RQ = 4
NA_WK = 3 * NA_RQ
NA_TQ = NA_RQ * GRID_W
NA_TK = NA_WK * GRID_W
NA_G = 16

DIFF_HEADS = 8
DIFF_HEAD_DIM = 64
DIFF_T = 512
T5_BUCKETS = 32
T5_MAX_DIST = 128
DIFF_NEAR = 2
DIFF_UNROLL = 8
DIFF_QSPLIT = 2
DIFF_QTILES = 2

HEAD_LANES = 128
VT_GROUPS = D_MODEL // HEAD_LANES
VT_ROWS = HEAD_LANES + 16
D_FF = 2816

TM_QKV = 512
TM_FFN = 512
VMEM_LIMIT = 56 * 1024 * 1024

_NT = (((1,), (1,)), ((), ()))
LOG2E = math.log2(math.e)


def _rms(x, g):
    return x * lax.rsqrt(jnp.mean(x * x, axis=-1, keepdims=True) + RMS_EPS) * g


def _proj_kernel(x_ref, g_ref, wqt_ref, wk_ref, wvt_ref, qt_ref, k_ref, vt_ref, *, tile):
    h = _rms(x_ref[...], g_ref[...]).astype(BF16)
    k_ref[...] = jnp.dot(h, wk_ref[...], preferred_element_type=F32).astype(BF16)
    qt = (lax.dot_general(wqt_ref[...], h, _NT, preferred_element_type=F32) * LOG2E).astype(BF16)
    vt = lax.dot_general(wvt_ref[...], h, _NT, preferred_element_type=F32).astype(BF16)
    ones = jnp.ones((VT_ROWS - HEAD_LANES, tile), BF16)
    for i in range(TM_QKV // tile):
        cols = slice(i * tile, (i + 1) * tile)
        for grp in range(VT_GROUPS):
            rows = slice(grp * HEAD_LANES, (grp + 1) * HEAD_LANES)
            qt_ref[i, grp] = qt[rows, cols]
            vt_ref[i, grp, :HEAD_LANES, :] = vt[rows, cols]
            vt_ref[i, grp, HEAD_LANES:, :] = ones


def _proj(x2, g, wqt, wk, wvt, tile):
    n = x2.shape[0]
    per_step = TM_QKV // tile
    fixed = lambda i: (0, 0)
    return pl.pallas_call(
        functools.partial(_proj_kernel, tile=tile),
        grid=(n // TM_QKV,),
        in_specs=[
            pl.BlockSpec((TM_QKV, D_MODEL), lambda i: (i, 0)),
            pl.BlockSpec((1, D_MODEL), fixed),
            pl.BlockSpec((D_MODEL, D_MODEL), fixed),
            pl.BlockSpec((D_MODEL, D_MODEL), fixed),
            pl.BlockSpec((D_MODEL, D_MODEL), fixed),
        ],
        out_specs=[
            pl.BlockSpec((per_step, VT_GROUPS, HEAD_LANES, tile), lambda i: (i, 0, 0, 0)),
            pl.BlockSpec((TM_QKV, D_MODEL), lambda i: (i, 0)),
            pl.BlockSpec((per_step, VT_GROUPS, VT_ROWS, tile), lambda i: (i, 0, 0, 0)),
        ],
        out_shape=[
            jax.ShapeDtypeStruct((n // tile, VT_GROUPS, HEAD_LANES, tile), BF16),
            jax.ShapeDtypeStruct((n, D_MODEL), BF16),
            jax.ShapeDtypeStruct((n // tile, VT_GROUPS, VT_ROWS, tile), BF16),
        ],
        compiler_params=pltpu.CompilerParams(
            dimension_semantics=("arbitrary",), vmem_limit_bytes=VMEM_LIMIT),
        name="qkv_proj",
    )(x2, g.reshape(1, D_MODEL), wqt, wk, wvt)


def _post_ffn_kernel(a_ref, x_ref, wo_ref, gpost_ref, gfpre_ref, gfpost_ref,
                     wg_ref, wu_ref, wd_ref, o_ref):
    halves = [slice(i * TM_FFN // 2, (i + 1) * TM_FFN // 2) for i in range(2)]
    x1s, hs = [], []
    for rows in halves:
        m = jnp.dot(a_ref[rows, :], wo_ref[...], preferred_element_type=F32)
        x1 = x_ref[rows, :] + _rms(m, gpost_ref[...])
        x1s.append(x1)
        hs.append(_rms(x1, gfpre_ref[...]).astype(BF16))
    for rows, x1, h in zip(halves, x1s, hs):
        gate = jnp.dot(h, wg_ref[...], preferred_element_type=F32)
        up = jnp.dot(h, wu_ref[...], preferred_element_type=F32)
        act = (gate * (1.0 / (1.0 + jnp.exp(-gate))) * up).astype(BF16)
        f = jnp.dot(act, wd_ref[...], preferred_element_type=F32)
        o_ref[rows, :] = x1 + _rms(f, gfpost_ref[...])


def _post_ffn(att2, x2, wo, gpost, gfpre, gfpost, wg, wu, wd):
    n = x2.shape[0]
    row = lambda i: (i, 0)
    fixed = lambda i: (0, 0)
    once = pl.Buffered(1)
    vec = pl.BlockSpec((1, D_MODEL), fixed)
    return pl.pallas_call(
        _post_ffn_kernel,
        grid=(n // TM_FFN,),
        in_specs=[
            pl.BlockSpec((TM_FFN, D_MODEL), row),
            pl.BlockSpec((TM_FFN, D_MODEL), row),
            pl.BlockSpec((D_MODEL, D_MODEL), fixed, pipeline_mode=once),
            vec, vec, vec,
            pl.BlockSpec((D_MODEL, D_FF), fixed, pipeline_mode=once),
            pl.BlockSpec((D_MODEL, D_FF), fixed, pipeline_mode=once),
            pl.BlockSpec((D_FF, D_MODEL), fixed, pipeline_mode=once),
        ],
        out_specs=pl.BlockSpec((TM_FFN, D_MODEL), row),
        out_shape=jax.ShapeDtypeStruct((n, D_MODEL), F32),
        compiler_params=pltpu.CompilerParams(
            dimension_semantics=("arbitrary",), vmem_limit_bytes=VMEM_LIMIT),
        name="post_ffn",
    )(att2, x2, wo, gpost.reshape(1, D_MODEL), gfpre.reshape(1, D_MODEL),
      gfpost.reshape(1, D_MODEL), wg, wu, wd)


def _na_bias_tiles(rpb):
    rpb = rpb.astype(F32) * LOG2E
    a = np.arange(NA_RQ)
    w = np.arange(NA_WK)
    qc = np.arange(GRID_W)
    kc = np.arange(GRID_W)
    q_start = np.clip(qc - NA_KW // 2, 0, GRID_W - NA_KW)
    col_ok = (kc[:, None] >= q_start[None, :]) & (kc[:, None] < q_start[None, :] + NA_KW)
    dc = np.clip(kc[:, None] - qc[None, :] + (NA_KW - 1), 0, 2 * NA_KW - 2)
    by_col = jnp.where(col_ok, rpb[:, :, dc], NEG_INF)
    masked_row = 2 * NA_KH - 1
    by_col = jnp.concatenate([by_col, jnp.full_like(by_col[:, :1], NEG_INF)], axis=1)
    offs = (0, NA_KH // 2, NA_WK - NA_RQ)
    los = (np.zeros_like(a), a, np.full_like(a, NA_WK - NA_KH))
    dr = np.stack([np.where((w[:, None] >= lo[None, :]) & (w[:, None] < lo[None, :] + NA_KH),
                            w[:, None] - (a[None, :] + off) + (NA_KH - 1), masked_row)
                   for off, lo in zip(offs, los)])
    t = by_col[:, dr]
    t = t.reshape(NA_HEADS // 2, 2, 3, NA_WK, NA_RQ, GRID_W, GRID_W)
    return jnp.transpose(t, (2, 0, 3, 5, 1, 4, 6)).reshape(3, NA_HEADS // 2, NA_TK, 2 * NA_TQ)


def _na_kernel(qt_ref, k_ref, vt_ref, bias_ref, o_ref, *, nblk, group):
    g = pl.program_id(2)
    first_rows = lax.broadcasted_iota(jnp.int32, (HEAD_LANES, NA_TQ), 0) < NA_HEAD_DIM
    ktiles = NA_TK // NA_TQ

    def window(u):
        blk = g * group + u
        return blk, jnp.clip(blk - 1, 0, nblk - ktiles)

    def scores(u, hh):
        blk, kt0 = window(u)
        cls = jnp.where(blk == 0, 0, jnp.where(blk == nblk - 1, 2, 1))
        qt = qt_ref[u, 0]
        zero = jnp.zeros_like(qt)
        qh = jnp.where(first_rows, qt, zero) if hh == 0 else jnp.where(first_rows, zero, qt)
        k = k_ref[0, pl.ds(pl.multiple_of(kt0 * NA_TQ, NA_TQ), NA_TK), :]
        return (jnp.dot(k, qh, preferred_element_type=F32)
                + bias_ref[cls, 0, :, hh * NA_TQ:(hh + 1) * NA_TQ])

    def attend(u, s):
        _, kt0 = window(u)
        vt = jnp.concatenate([vt_ref[kt0 + i, 0] for i in range(ktiles)], axis=1)
        p = jnp.exp2(s - jnp.max(s, axis=0, keepdims=True)).astype(BF16)
        return jnp.dot(vt, p, preferred_element_type=F32)

    def write(u, o):
        num = jnp.where(first_rows, o[0][:HEAD_LANES], o[1][:HEAD_LANES])
        den = jnp.where(first_rows, o[0][HEAD_LANES:HEAD_LANES + 1], o[1][HEAD_LANES:HEAD_LANES + 1])
        o_ref[0, u * NA_TQ:(u + 1) * NA_TQ, :] = jnp.transpose(num / den).astype(BF16)

    pending = [scores(0, hh) for hh in range(2)]
    for u in range(1, group):
        outs = []
        for hh in range(2):
            ahead = scores(u, hh)
            outs.append(attend(u - 1, pending[hh]))
            pending[hh] = ahead
        write(u - 1, outs)
    write(group - 1, [attend(group - 1, pending[hh]) for hh in range(2)])


def _na_attention(qt, k3, vt, bias):
    b, s, _ = k3.shape
    nblk = s // NA_TQ
    group = math.gcd(NA_G, nblk)
    steps = nblk // group
    assert nblk >= NA_TK // NA_TQ and s % NA_TQ == 0
    npair = NA_HEADS // 2
    return pl.pallas_call(
        functools.partial(_na_kernel, nblk=nblk, group=group),
        grid=(npair, b, steps),
        in_specs=[
            pl.BlockSpec((group, 1, HEAD_LANES, NA_TQ), lambda hp, bb, i: (bb * steps + i, hp, 0, 0)),
            pl.BlockSpec((1, s, HEAD_LANES), lambda hp, bb, i: (bb, 0, hp)),
            pl.BlockSpec((nblk, 1, VT_ROWS, NA_TQ), lambda hp, bb, i: (bb, hp, 0, 0)),
            pl.BlockSpec((3, 1, NA_TK, 2 * NA_TQ), lambda hp, bb, i: (0, hp, 0, 0)),
        ],
        out_specs=pl.BlockSpec((1, group * NA_TQ, HEAD_LANES), lambda hp, bb, i: (bb, i, hp)),
        out_shape=jax.ShapeDtypeStruct((b, s, D_MODEL), BF16),
        compiler_params=pltpu.CompilerParams(
            dimension_semantics=("arbitrary", "arbitrary", "arbitrary"),
            vmem_limit_bytes=VMEM_LIMIT),
        name="na_attention",
    )(qt, k3, vt, bias)


def _t5_bucket(rel):
    nb = T5_BUCKETS // 2
    max_exact = nb // 2
    ret = jnp.where(rel > 0, nb, 0)
    n = jnp.abs(rel)
    nf = jnp.maximum(n, max_exact).astype(F32)
    large = max_exact + (jnp.log(nf / max_exact) / math.log(T5_MAX_DIST / max_exact)
                         * (nb - max_exact)).astype(jnp.int32)
    large = jnp.minimum(large, nb - 1)
    return ret + jnp.where(n < max_exact, n, large)


def _diff_bias_tiles(t5_bias):
    table = t5_bias.astype(F32)
    t, blk = DIFF_T, T5_MAX_DIST
    nh = DIFF_HEADS
    i = jnp.arange(2 * blk)
    cs = jnp.array([-blk, 0, blk])
    rel = cs[:, None] + (blk - 1) - i[None, :]
    u = jnp.transpose(table[_t5_bucket(rel)], (2, 0, 1))
    skew = jnp.broadcast_to(u[:, :, None, :], (nh, 3, blk, 2 * blk)).reshape(nh, 3, 2 * blk * blk)
    skew = skew[:, :, :blk * (2 * blk - 1)].reshape(nh, 3, blk, 2 * blk - 1)[:, :, :, blk - 1:]
    left = jnp.broadcast_to(table[T5_BUCKETS // 2 - 1][:, None, None, None], (nh, 1, blk, blk))
    right = jnp.broadcast_to(table[T5_BUCKETS - 1][:, None, None, None], (nh, 1, blk, blk))
    blocks = jnp.concatenate([left, skew, right], axis=1)
    nb = t // blk
    d = np.arange(-DIFF_NEAR, DIFF_NEAR + 1)[:, None, None]
    c = d * t + blk * (np.arange(nb)[None, :, None] - np.arange(nb)[None, None, :])
    which = np.clip(c // blk + 2, 0, 4)
    tiles = blocks[:, which]
    return jnp.transpose(tiles, (0, 1, 2, 4, 3, 5)).reshape(nh, 2 * DIFF_NEAR + 1, t, t)


def _diff_query_tile(sub, far_ref, lam_ref, qt_ref, k_ref, vt_ref, nb_ref, g_ref, o_ref, s_even, s_odd,
                     acc_scr, *, nk, post_scale):
    t = DIFF_T
    h = pl.program_id(0)
    qi = pl.program_id(2) * DIFF_QTILES + sub
    qt = qt_ref[sub, 0]
    first = lax.broadcasted_iota(jnp.int32, (HEAD_LANES, t), 0) < DIFF_HEAD_DIM
    zero = jnp.zeros_like(qt)
    qs = (jnp.where(first, qt, zero), jnp.where(first, zero, qt))
    c_left = far_ref[h, 0]
    c_right = far_ref[h, 1]

    s_slots = (s_even, s_odd)

    parts = [(c, slice(h * t // DIFF_QSPLIT, (h + 1) * t // DIFF_QSPLIT))
             for c in range(2) for h in range(DIFF_QSPLIT)]

    def scores(j, slot, part, bias_tile, bias_const):
        c, cols = part
        k = k_ref[0, pl.ds(pl.multiple_of(j * t, t), t), :]
        s = jnp.dot(k, qs[c][:, cols], preferred_element_type=F32)
        if bias_tile is not None:
            s = s + bias_tile[:, cols]
        s_slots[slot][c, :, cols] = s
        cm = jnp.max(s, axis=0, keepdims=True)
        return cm if bias_const is None else cm + bias_const

    def consume(j, slot, part, cm, bias_const, m_old):
        c, cols = part
        vt = vt_ref[j, 0]
        m_new = jnp.maximum(m_old, cm)
        p = jnp.exp2(s_slots[slot][c, :, cols] - (m_new - bias_const)).astype(BF16)
        alpha = jnp.exp2(m_old - m_new)
        acc_scr[sub, c, :, cols] = alpha * acc_scr[sub, c, :, cols] + jnp.dot(vt, p, preferred_element_type=F32)
        return m_new

    def step(j, slot, state, bias_tile, bias_const):
        cm_prev, c_prev, m_prev = state
        cms, ms = [], []
        for i, part in enumerate(parts):
            cms.append(scores(j, slot, part, bias_tile, bias_const))
            ms.append(consume(j - 1, 1 - slot, part, cm_prev[i], c_prev, m_prev[i]))
        c_new = jnp.float32(0.0) if bias_const is None else bias_const
        return (tuple(cms), c_new, tuple(ms))

    def tile_loop(start, stop, state, near, bias_const):
        def run(first, trips, width, st0):
            def body(i, st):
                j = first + width * i
                for d in range(width):
                    tile = nb_ref[0, j + d - qi + DIFF_NEAR] if near else None
                    st = step(j + d, (1 + d) % 2, st, tile, bias_const)
                return st
            return lax.fori_loop(0, trips, body, st0)
        for width in ((2 * DIFF_NEAR, 2) if near else (DIFF_UNROLL, 2 * DIFF_NEAR, 2)):
            trips = (stop - start) // width
            state = run(start, trips, width, state)
            start = start + width * trips
        return state

    m0 = jnp.full((1, t // DIFF_QSPLIT), NEG_INF, F32)
    acc_scr[sub] = jnp.zeros_like(acc_scr[sub])
    last = nk - 1
    near_lo = qi - 1 - (qi % 2)
    near_hi = near_lo + 2 * DIFF_NEAR
    bias_first = jnp.where(near_lo <= 0, nb_ref[0, jnp.clip(DIFF_NEAR - qi, 0, 2 * DIFF_NEAR)], c_left)
    state = (tuple(scores(0, 0, part, bias_first, None) for part in parts), jnp.float32(0.0),
             (m0,) * len(parts))
    near_start = jnp.maximum(near_lo, 1)
    near_stop = jnp.minimum(near_hi, last)
    state = tile_loop(1, near_start, state, False, c_left)
    state = tile_loop(near_start, near_stop, state, True, None)
    state = tile_loop(near_stop, last, state, False, c_right)
    bias_last = jnp.where(near_hi > last,
                          nb_ref[0, jnp.clip(last - qi + DIFF_NEAR, 0, 2 * DIFF_NEAR)], c_right)
    state = step(last, 1, state, bias_last, None)
    cm_prev, c_prev, m_prev = state
    for i, part in enumerate(parts):
        consume(last, 1, part, cm_prev[i], c_prev, m_prev[i])
    num = [acc_scr[sub, c, :HEAD_LANES, :] for c in range(2)]
    den = [acc_scr[sub, c, HEAD_LANES:HEAD_LANES + 1, :] for c in range(2)]
    out = num[0] / den[0] - lam_ref[0] * (num[1] / den[1])
    out = out * lax.rsqrt(jnp.mean(out * out, axis=0, keepdims=True) + RMS_EPS) * g_ref[...]
    o_ref[0, sub * t:(sub + 1) * t, :] = jnp.transpose(out * post_scale).astype(BF16)


def _diff_kernel(*refs, nk, post_scale):
    for sub in range(DIFF_QTILES):
        _diff_query_tile(sub, *refs, nk=nk, post_scale=post_scale)


def _diff_attention(qt, k3, vt, near_bias, far_bias, lam, subln_g, lambda_init):
    b, s, _ = k3.shape
    t = DIFF_T
    nk = s // t
    assert nk >= 2 and nk % 2 == 0 and nk % DIFF_QTILES == 0 and t > T5_MAX_DIST
    steps = nk // DIFF_QTILES
    nh = DIFF_HEADS
    grid_spec = pltpu.PrefetchScalarGridSpec(
        num_scalar_prefetch=2,
        grid=(nh, b, steps),
        in_specs=[
            pl.BlockSpec((DIFF_QTILES, 1, HEAD_LANES, t), lambda h, bb, i, *_: (bb * steps + i, h, 0, 0)),
            pl.BlockSpec((1, s, HEAD_LANES), lambda h, bb, i, *_: (bb, 0, h)),
            pl.BlockSpec((nk, 1, VT_ROWS, t), lambda h, bb, i, *_: (bb, h, 0, 0)),
            pl.BlockSpec((1, 2 * DIFF_NEAR + 1, t, t), lambda h, bb, i, *_: (h, 0, 0, 0)),
            pl.BlockSpec((HEAD_LANES, 1), lambda h, bb, i, *_: (0, 0)),
        ],
        out_specs=pl.BlockSpec((1, DIFF_QTILES * t, HEAD_LANES), lambda h, bb, i, *_: (bb, i, h)),
        scratch_shapes=[
            pltpu.VMEM((2, t, t), F32),
            pltpu.VMEM((2, t, t), F32),
            pltpu.VMEM((DIFF_QTILES, 2, VT_ROWS, t), F32),
        ],
    )
    return pl.pallas_call(
        functools.partial(_diff_kernel, nk=nk, post_scale=1.0 - lambda_init),
        grid_spec=grid_spec,
        out_shape=jax.ShapeDtypeStruct((b, s, D_MODEL), BF16),
        compiler_params=pltpu.CompilerParams(
            dimension_semantics=("arbitrary", "arbitrary", "arbitrary"),
            vmem_limit_bytes=VMEM_LIMIT),
        name="diff_attention",
    )(far_bias, lam, qt, k3, vt, near_bias, subln_g.reshape(HEAD_LANES, 1))


def _split_qkv(w, head_dim):
    scale = head_dim ** -0.5
    assert math.frexp(scale)[0] == 0.5
    d = D_MODEL
    return (jnp.transpose(w[:, :d] * scale).astype(BF16), w[:, d:2 * d].astype(BF16),
            jnp.transpose(w[:, 2 * d:]).astype(BF16))


def kernel(x_prompt, x_sample, mix_pre_g, mix_post_g, na_w_qkv, na_w_o, na_rpb, diff_w_qkv, diff_w_o,
           diff_lambda_q1, diff_lambda_k1, diff_lambda_q2, diff_lambda_k2, diff_subln_g, t5_bias,
           ffn_pre_g, ffn_post_g, ffn_w_gate, ffn_w_up, ffn_w_down):
    na_bias = [_na_bias_tiles(na_rpb[li]) for li in range(na_rpb.shape[0])]
    table = t5_bias.astype(F32) * LOG2E
    diff_near = _diff_bias_tiles(table)
    far = jnp.stack([table[T5_BUCKETS // 2 - 1], table[T5_BUCKETS - 1]], axis=1)
    wg = ffn_w_gate.astype(BF16)
    wu = ffn_w_up.astype(BF16)
    wd = ffn_w_down.astype(BF16)

    def trunk(x):
        b, s, d = x.shape
        x2 = x.reshape(b * s, d)
        for i in range(DEPTH):
            li = i // 2
            if i % 2 == 0:
                qt, k, vt = _proj(x2, mix_pre_g[i], *_split_qkv(na_w_qkv[li], NA_HEAD_DIM), NA_TQ)
                att = _na_attention(qt, k.reshape(b, s, d), vt, na_bias[li])
                w_o = na_w_o[li]
            else:
                lambda_init = 0.8 - 0.6 * math.exp(-0.3 * i)
                lam = (jnp.exp(jnp.sum(diff_lambda_q1[li].astype(F32) * diff_lambda_k1[li].astype(F32)))
                       - jnp.exp(jnp.sum(diff_lambda_q2[li].astype(F32) * diff_lambda_k2[li].astype(F32)))
                       + lambda_init).reshape(1)
                qt, k, vt = _proj(x2, mix_pre_g[i], *_split_qkv(diff_w_qkv[li], DIFF_HEAD_DIM), DIFF_T)
                att = _diff_attention(qt, k.reshape(b, s, d), vt, diff_near, far, lam,
                                      diff_subln_g[li], lambda_init)
                w_o = diff_w_o[li]
            x2 = _post_ffn(att.reshape(b * s, d), x2, w_o.astype(BF16), mix_post_g[i],
                           ffn_pre_g[i], ffn_post_g[i], wg[i], wu[i], wd[i])
        return x2.reshape(b, s, d)

    return (trunk(x_prompt), trunk(x_sample))
```
